```python
import math
import jax, jax.numpy as jnp
from jax import lax
import numpy as np

D_MODEL = 1024
BATCH = 4
SEQ = 4096
DEPTH = 2

HEAD_DIM = 64
A_HEADS = 8
B_HEADS = 4
DILATED_CONFIGS = ((128, 1), (512, 4), (2048, 16))
MAX_WINDOW = 2048
Q_BLOCK = 128
A_WIDTH = A_HEADS * HEAD_DIM
B_QK_WIDTH = B_HEADS * 2 * HEAD_DIM
B_V_WIDTH = B_HEADS * 2 * HEAD_DIM
ATTN_IN = 3 * A_WIDTH + 2 * B_QK_WIDTH + B_V_WIDTH
ATTN_OUT = A_WIDTH + B_V_WIDTH
ATTN_SPLITS = (A_WIDTH, 2 * A_WIDTH, 3 * A_WIDTH, 3 * A_WIDTH + B_QK_WIDTH, 3 * A_WIDTH + 2 * B_QK_WIDTH)
LRU_WIDTH = D_MODEL
LRU_BLOCKS = 8
LRU_BLOCK_WIDTH = LRU_WIDTH // LRU_BLOCKS
LRU_C = 8.0
REC_CONV = 4
FFN_DIM = 3 * D_MODEL
FFN_CONV = 3

N_EVEN = (DEPTH + 1) // 2
N_ODD = DEPTH // 2
NORM_EPS = 1e-6
NEG_INF = -1e30

kernel_name = "hybrid_dilated_diff_rglru_convffn"


def rms_norm(x, g):
    xf = x.astype(jnp.float32)
    y = xf * lax.rsqrt(jnp.mean(xf * xf, axis=-1, keepdims=True) + NORM_EPS)
    return (y * g.astype(jnp.float32)).astype(x.dtype)


def alibi_slopes(n):
    return jnp.exp2(-8.0 * jnp.arange(1, n + 1, dtype=jnp.float32) / n)


def causal_depthwise_conv(x, w, b):
    K, C = w.shape
    y = lax.conv_general_dilated(
        x, w.astype(x.dtype)[:, None, :], window_strides=(1,), padding=((K - 1, 0),),
        dimension_numbers=('NWC', 'WIO', 'NWC'), feature_group_count=C)
    return y + b.astype(x.dtype)


def dilated_branch(q, k, v, slopes, window, dilation):
    B, S, H, E = q.shape
    n = window // dilation
    L = S // dilation
    nb = L // n

    def to_blocks(t):
        t = t.reshape(B, L, dilation, H, E).transpose(0, 2, 3, 1, 4)
        return t.reshape(B, dilation, H, nb, n, E)

    def with_prev(t):
        prev = jnp.pad(t[:, :, :, :-1], ((0, 0), (0, 0), (0, 0), (1, 0), (0, 0), (0, 0)))
        return jnp.concatenate([prev, t], axis=4)

    qb = to_blocks(q)
    kb = with_prev(to_blocks(k))
    vb = with_prev(to_blocks(v))
    s = jnp.einsum('bdhnqe,bdhnke->bdhnqk', qb, kb) * (E ** -0.5)
    kj = jnp.arange(2 * n)[None, :]
    delta = jnp.arange(n)[:, None] + n - kj
    in_band = (delta >= 0) & (delta <= n)
    after_start = (jnp.arange(nb)[:, None, None] > 0) | (kj[None] >= n)
    valid = in_band[None] & after_start
    dist = (delta * dilation).astype(jnp.float32)
    s = jnp.where(valid, s - slopes[:, None, None, None] * dist, NEG_INF)
    m = jnp.max(s, axis=-1, keepdims=True)
    e = jnp.exp(s - m)
    den = jnp.sum(e, axis=-1)
    o = jnp.einsum('bdhnqk,bdhnke->bdhnqe', e, vb) / den[..., None]
    lse = m[..., 0] + jnp.log(den)
    o = o.reshape(B, dilation, H, L, E).transpose(0, 3, 1, 2, 4).reshape(B, S, H, E)
    lse = lse.reshape(B, dilation, H, L).transpose(0, 3, 1, 2).reshape(B, S, H)
    return o, lse


def dilated_mixture_attention(q, k, v, slopes):
    B, S, H, E = q.shape
    s_pad = -(-S // MAX_WINDOW) * MAX_WINDOW
    pad = ((0, 0), (0, s_pad - S), (0, 0), (0, 0))
    qf, kf, vf = [jnp.pad(t.astype(jnp.float32), pad) for t in (q, k, v)]
    outs, lses = zip(*[dilated_branch(qf, kf, vf, slopes, w, d) for w, d in DILATED_CONFIGS])
    weights = jax.nn.softmax(jnp.stack(lses), axis=0)
    o = jnp.sum(weights[..., None] * jnp.stack(outs), axis=0)
    return o[:, :S]


def differential_attention(q, k, v, slopes, lam):
    B, S, H, _, E = q.shape
    n_blocks = S // Q_BLOCK
    kf = k.astype(jnp.float32)
    vf = v.astype(jnp.float32)
    kpos = jnp.arange(S)
    sl = slopes[:, None, None, None]

    def one_block(i):
        qs = lax.dynamic_slice_in_dim(q, i * Q_BLOCK, Q_BLOCK, axis=1).astype(jnp.float32)
        s = jnp.einsum('bqhce,bkhce->bhcqk', qs, kf) * (E ** -0.5)
        dist = (i * Q_BLOCK + jnp.arange(Q_BLOCK))[:, None] - kpos[None, :]
        s = jnp.where(dist >= 0, s - sl * dist.astype(jnp.float32), NEG_INF)
        p = jax.nn.softmax(s, axis=-1)
        attn = p[:, :, 0] - lam * p[:, :, 1]
        return jnp.einsum('bhqk,bkhe->bqhe', attn, vf)

    o = lax.map(one_block, jnp.arange(n_blocks))
    return jnp.moveaxis(o, 0, 1).reshape(B, S, H, 2 * E)


def hybrid_attention_mixer(x, w_in, w_out, a_q_norm, a_k_norm, b_q_norm, b_k_norm, b_sub_norm,
                           lam_q1, lam_k1, lam_q2, lam_k2, lam_init):
    B, S, _ = x.shape
    proj = x @ w_in.astype(x.dtype)
    qa, ka, va, qb, kb, vb = jnp.split(proj, list(ATTN_SPLITS), axis=-1)
    qa = rms_norm(qa.reshape(B, S, A_HEADS, HEAD_DIM), a_q_norm)
    ka = rms_norm(ka.reshape(B, S, A_HEADS, HEAD_DIM), a_k_norm)
    va = va.reshape(B, S, A_HEADS, HEAD_DIM)
    qb = rms_norm(qb.reshape(B, S, B_HEADS, 2, HEAD_DIM), b_q_norm)
    kb = rms_norm(kb.reshape(B, S, B_HEADS, 2, HEAD_DIM), b_k_norm)
    vb = vb.reshape(B, S, B_HEADS, 2 * HEAD_DIM)
    slopes = alibi_slopes(A_HEADS + B_HEADS)
    oa = dilated_mixture_attention(qa, ka, va, slopes[:A_HEADS])
    f32 = jnp.float32
    lam = (jnp.exp(jnp.sum(lam_q1.astype(f32) * lam_k1.astype(f32)))
           - jnp.exp(jnp.sum(lam_q2.astype(f32) * lam_k2.astype(f32))) + lam_init)
    ob = differential_attention(qb, kb, vb, slopes[A_HEADS:], lam)
    ob = rms_norm(ob, b_sub_norm) * (1.0 - lam_init)
    y = jnp.concatenate([oa.reshape(B, S, A_WIDTH), ob.reshape(B, S, B_V_WIDTH)], axis=-1).astype(x.dtype)
    return y @ w_out.astype(x.dtype)


def linear_combine(left, right):
    a_l, b_l = left
    a_r, b_r = right
    return a_l * a_r, a_r * b_l + b_r


def rg_lru(x, gate_a_w, gate_a_b, gate_x_w, gate_x_b, a_param):
    B, S, C = x.shape
    xb = x.reshape(B, S, LRU_BLOCKS, LRU_BLOCK_WIDTH)

    def block_diag(w, b):
        y = jnp.einsum('bsnc,ncd->bsnd', xb, w.astype(x.dtype)).reshape(B, S, C) + b.astype(x.dtype)
        return y.astype(jnp.float32)

    r = jax.nn.sigmoid(block_diag(gate_a_w, gate_a_b))
    i = jax.nn.sigmoid(block_diag(gate_x_w, gate_x_b))
    log_a = -LRU_C * r * jax.nn.softplus(-a_param.astype(jnp.float32))
    a = jnp.exp(log_a)
    u = jnp.sqrt(-jnp.expm1(2.0 * log_a)) * (i * x.astype(jnp.float32))
    _, h = lax.associative_scan(linear_combine, (a, u), axis=1)
    return h.astype(x.dtype)


def recurrent_mixer(x, w_in, conv_w, conv_b, gate_a_w, gate_a_b, gate_x_w, gate_x_b, a_param, w_out):
    gate, xr = jnp.split(x @ w_in.astype(x.dtype), [LRU_WIDTH], axis=-1)
    xr = causal_depthwise_conv(xr, conv_w, conv_b)
    h = rg_lru(xr, gate_a_w, gate_a_b, gate_x_w, gate_x_b, a_param)
    return (h * jax.nn.gelu(gate)) @ w_out.astype(x.dtype)


def conv_ffn(x, w_up, conv_w, conv_b, w_down):
    u = causal_depthwise_conv(x @ w_up.astype(x.dtype), conv_w, conv_b)
    g, val = jnp.split(u, [FFN_DIM], axis=-1)
    return (jax.nn.gelu(g) * val) @ w_down.astype(x.dtype)


def setup_inputs(seed: int = 0) -> dict:
    key = jax.random.key(seed)
    ks = iter(jax.random.split(key, 40))
    f32 = jnp.float32

    def normal(shape, scale):
        return scale * jax.random.normal(next(ks), shape, f32)

    def gain(shape):
        return 1.0 + 0.02 * jax.random.normal(next(ks), shape, f32)

    u = jax.random.uniform(next(ks), (N_ODD, LRU_WIDTH), f32, 0.9, 0.999)
    a0 = u ** (1.0 / LRU_C)
    a_param = jnp.log(a0) - jnp.log1p(-a0)
    return {
        'x': normal((BATCH, SEQ, D_MODEL), 1.0),
        'attn_norm': gain((N_EVEN, D_MODEL)),
        'attn_w_in': normal((N_EVEN, D_MODEL, ATTN_IN), D_MODEL ** -0.5),
        'attn_w_out': normal((N_EVEN, ATTN_OUT, D_MODEL), ATTN_OUT ** -0.5),
        'a_q_norm': gain((N_EVEN, HEAD_DIM)),
        'a_k_norm': gain((N_EVEN, HEAD_DIM)),
        'b_q_norm': gain((N_EVEN, HEAD_DIM)),
        'b_k_norm': gain((N_EVEN, HEAD_DIM)),
        'b_sub_norm': gain((N_EVEN, 2 * HEAD_DIM)),
        'b_lam_q1': normal((N_EVEN, HEAD_DIM), 0.1),
        'b_lam_k1': normal((N_EVEN, HEAD_DIM), 0.1),
        'b_lam_q2': normal((N_EVEN, HEAD_DIM), 0.1),
        'b_lam_k2': normal((N_EVEN, HEAD_DIM), 0.1),
        'rec_norm': gain((N_ODD, D_MODEL)),
        'rec_w_in': normal((N_ODD, D_MODEL, 2 * LRU_WIDTH), D_MODEL ** -0.5),
        'rec_conv_w': normal((N_ODD, REC_CONV, LRU_WIDTH), REC_CONV ** -0.5),
        'rec_conv_b': normal((N_ODD, LRU_WIDTH), 0.01),
        'rec_gate_a_w': normal((N_ODD, LRU_BLOCKS, LRU_BLOCK_WIDTH, LRU_BLOCK_WIDTH), LRU_BLOCK_WIDTH ** -0.5),
        'rec_gate_a_b': normal((N_ODD, LRU_WIDTH), 0.01),
        'rec_gate_x_w': normal((N_ODD, LRU_BLOCKS, LRU_BLOCK_WIDTH, LRU_BLOCK_WIDTH), LRU_BLOCK_WIDTH ** -0.5),
        'rec_gate_x_b': normal((N_ODD, LRU_WIDTH), 0.01),
        'rec_a_param': a_param,
        'rec_w_out': normal((N_ODD, LRU_WIDTH, D_MODEL), LRU_WIDTH ** -0.5),
        'ffn_norm': gain((DEPTH, D_MODEL)),
        'ffn_w_up': normal((DEPTH, D_MODEL, 2 * FFN_DIM), D_MODEL ** -0.5),
        'ffn_conv_w': normal((DEPTH, FFN_CONV, 2 * FFN_DIM), FFN_CONV ** -0.5),
        'ffn_conv_b': normal((DEPTH, 2 * FFN_DIM), 0.01),
        'ffn_w_down': normal((DEPTH, FFN_DIM, D_MODEL), FFN_DIM ** -0.5),
    }


def reference(x, attn_norm, attn_w_in, attn_w_out, a_q_norm, a_k_norm, b_q_norm, b_k_norm, b_sub_norm,
              b_lam_q1, b_lam_k1, b_lam_q2, b_lam_k2, rec_norm, rec_w_in, rec_conv_w, rec_conv_b,
              rec_gate_a_w, rec_gate_a_b, rec_gate_x_w, rec_gate_x_b, rec_a_param, rec_w_out,
              ffn_norm, ffn_w_up, ffn_conv_w, ffn_conv_b, ffn_w_down):
    h = x
    for layer in range(DEPTH):
        j = layer // 2
        if layer % 2 == 0:
            lam_init = 0.8 - 0.6 * math.exp(-0.3 * layer)
            h = h + hybrid_attention_mixer(
                rms_norm(h, attn_norm[j]), attn_w_in[j], attn_w_out[j], a_q_norm[j], a_k_norm[j],
                b_q_norm[j], b_k_norm[j], b_sub_norm[j], b_lam_q1[j], b_lam_k1[j], b_lam_q2[j], b_lam_k2[j],
                lam_init)
        else:
            h = h + recurrent_mixer(
                rms_norm(h, rec_norm[j]), rec_w_in[j], rec_conv_w[j], rec_conv_b[j], rec_gate_a_w[j],
                rec_gate_a_b[j], rec_gate_x_w[j], rec_gate_x_b[j], rec_a_param[j], rec_w_out[j])
        h = h + conv_ffn(rms_norm(h, ffn_norm[layer]), ffn_w_up[layer], ffn_conv_w[layer],
                         ffn_conv_b[layer], ffn_w_down[layer])
    return h
```

```python
import functools
import math

import numpy as np
import jax
import jax.numpy as jnp
from jax import lax
from jax.experimental import pallas as pl
from jax.experimental.pallas import tpu as pltpu

F32 = jnp.float32
BF16 = jnp.bfloat16

HEAD_DIM = 64
A_HEADS = 8
B_HEADS = 4
DILATED_CONFIGS = ((128, 1), (512, 4), (2048, 16))
BAND = 128
MAX_WINDOW = 2048
LRU_BLOCKS = 8
LRU_C = 8.0
NORM_EPS = 1e-6
NEG_INF = -1e30
LANES = 128
SUBLANES = 8
VMEM_LIMIT = 56 * 1024 * 1024


def _gelu(x):
    c = math.sqrt(2.0 / math.pi)
    return x * (0.5 * (1.0 + jnp.tanh(c * (x + 0.044715 * (x * x * x)))))


def _sigmoid(x):
    return 1.0 / (1.0 + jnp.exp(-x))


def _rms_rows(x, g):
    ms = jnp.mean(x * x, axis=-1, keepdims=True)
    return x * lax.rsqrt(ms + NORM_EPS) * g


def _dot(a, b):
    return jnp.dot(a, b, preferred_element_type=F32)


def _dot_nt(a, b):
    return lax.dot_general(a, b, (((1,), (1,)), ((), ())), preferred_element_type=F32)


def _params(sem):
    return pltpu.CompilerParams(dimension_semantics=sem, vmem_limit_bytes=VMEM_LIMIT)


SEC = 512
N_SEC = 6


def _inproj_kernel(x_ref, g_ref, w_ref, hg_ref, p_ref, o_ref):
    xn = _rms_rows(x_ref[...], g_ref[...]).astype(BF16)
    for s in range(N_SEC):
        y = _dot(xn, w_ref[:, s * SEC:(s + 1) * SEC])
        if s % 3 == 2:
            o_ref[:, s * SEC:(s + 1) * SEC] = y.astype(BF16)
        else:
            ms = _dot((y * y).astype(BF16), p_ref[...])
            o_ref[:, s * SEC:(s + 1) * SEC] = (
                y * lax.rsqrt(ms + NORM_EPS) * hg_ref[s:s + 1, :]).astype(BF16)


def _attn_inproj(x2, g, w, head_gains, tm=512):
    T, D = x2.shape
    N = w.shape[1]
    blk = np.kron(np.eye(SEC // HEAD_DIM), np.full((HEAD_DIM, HEAD_DIM), 1.0 / HEAD_DIM))
    pmat = jnp.asarray(blk, dtype=BF16)
    return pl.pallas_call(
        _inproj_kernel,
        grid=(T // tm,),
        in_specs=[
            pl.BlockSpec((tm, D), lambda i: (i, 0)),
            pl.BlockSpec((1, D), lambda i: (0, 0)),
            pl.BlockSpec((D, N), lambda i: (0, 0)),
            pl.BlockSpec((N_SEC, SEC), lambda i: (0, 0)),
            pl.BlockSpec((SEC, SEC), lambda i: (0, 0)),
        ],
        out_specs=pl.BlockSpec((tm, N), lambda i: (i, 0)),
        out_shape=jax.ShapeDtypeStruct((T, N), BF16),
        compiler_params=_params(("parallel",)),
        name="attn_inproj",
    )(x2, g, w, head_gains, pmat)


def _dilated_kernel(slopes_ref, q_ref, k_ref, v_ref, o_ref, qf, kf, vf, ob, lb):
    S = q_ref.shape[0]
    pad = kf.shape[0] - S
    g = pl.program_id(1)
    qf[...] = q_ref[...].astype(F32)
    kf[0:pad, :] = jnp.zeros((pad, LANES), F32)
    vf[0:pad, :] = jnp.zeros((pad, LANES), F32)
    kf[pad:, :] = k_ref[...].astype(F32)
    vf[pad:, :] = v_ref[...].astype(F32)

    lo = lax.broadcasted_iota(jnp.int32, (BAND, LANES), 1) < HEAD_DIM
    ii = lax.broadcasted_iota(jnp.int32, (BAND, 2 * BAND), 0)
    jj = lax.broadcasted_iota(jnp.int32, (BAND, 2 * BAND), 1)
    delta = ii + BAND - jj
    in_band = (delta >= 0) & (delta <= BAND)
    prev_half = lax.broadcasted_iota(jnp.int32, (2 * BAND, 2 * BAND), 1) < BAND
    sl0 = slopes_ref[2 * g]
    sl1 = slopes_ref[2 * g + 1]

    for bi, (window, d) in enumerate(DILATED_CONFIGS):
        span = BAND * d
        nbs = S // span
        dist = (delta * d).astype(F32)
        bias = jnp.concatenate(
            [jnp.where(in_band, -sl0 * dist, NEG_INF),
             jnp.where(in_band, -sl1 * dist, NEG_INF)], axis=0)

        def body(blk, carry, d=d, span=span, nbs=nbs, bias=bias, bi=bi):
            r = blk // nbs
            nb = blk - r * nbs
            start = nb * span + r
            q = qf[pl.ds(start, BAND, stride=d), :].astype(BF16)
            k = kf[pl.ds(pad + start - span, 2 * BAND, stride=d), :].astype(BF16)
            v = vf[pl.ds(pad + start - span, 2 * BAND, stride=d), :].astype(BF16)
            zero = jnp.zeros_like(q)
            q2 = jnp.concatenate([jnp.where(lo, q, zero), jnp.where(lo, zero, q)], axis=0)
            s = _dot_nt(q2, k) + bias
            s = jnp.where((nb == 0) & prev_half, NEG_INF, s)
            m = jnp.max(s, axis=1, keepdims=True)
            e = jnp.exp(s - m)
            den = jnp.sum(e, axis=1, keepdims=True)
            o2 = _dot(e.astype(BF16), v) / den
            lse = m + jnp.log(den)
            o = jnp.where(lo, o2[:BAND], o2[BAND:])
            l = jnp.where(lo, jnp.broadcast_to(lse[:BAND], (BAND, LANES)),
                          jnp.broadcast_to(lse[BAND:], (BAND, LANES)))
            ob[bi, pl.ds(start, BAND, stride=d), :] = o
            lb[bi, pl.ds(start, BAND, stride=d), :] = l
            return carry

        lax.fori_loop(0, S // BAND, body, 0)

    rows = 512
    def mix(c, carry):
        sl = pl.ds(pl.multiple_of(c * rows, rows), rows)
        l0, l1, l2 = lb[0, sl, :], lb[1, sl, :], lb[2, sl, :]
        m = jnp.maximum(jnp.maximum(l0, l1), l2)
        w0, w1, w2 = jnp.exp(l0 - m), jnp.exp(l1 - m), jnp.exp(l2 - m)
        tot = w0 + w1 + w2
        o = (w0 * ob[0, sl, :] + w1 * ob[1, sl, :] + w2 * ob[2, sl, :]) / tot
        o_ref[sl, :] = o.astype(o_ref.dtype)
        return carry
    lax.fori_loop(0, S // rows, mix, 0)


def _dilated_attention(proj3, slopes):
    B, S, _ = proj3.shape
    assert S % MAX_WINDOW == 0
    pairs = A_HEADS // 2
    blk = lambda off: pl.BlockSpec((None, S, LANES), lambda b, g, off=off: (b, 0, off + g))
    return pl.pallas_call(
        _dilated_kernel,
        grid=(B, pairs),
        in_specs=[
            pl.BlockSpec(memory_space=pltpu.SMEM),
            blk(0), blk(pairs), blk(2 * pairs),
        ],
        out_specs=pl.BlockSpec((None, S, LANES), lambda b, g: (b, 0, g)),
        out_shape=jax.ShapeDtypeStruct((B, S, A_HEADS * HEAD_DIM), BF16),
        scratch_shapes=[
            pltpu.VMEM((S, LANES), F32),
            pltpu.VMEM((MAX_WINDOW + S, LANES), F32),
            pltpu.VMEM((MAX_WINDOW + S, LANES), F32),
            pltpu.VMEM((3, S, LANES), F32),
            pltpu.VMEM((3, S, LANES), F32),
        ],
        compiler_params=_params(("parallel", "parallel")),
        name="dilated_attn",
    )(slopes, proj3, proj3, proj3)


TQ = 256


def _diff_kernel(slopes_ref, q_ref, k_ref, v_ref, lam_ref, sg_ref, o_ref,
                 m_ref, l_ref, acc_ref, *, lam_init):
    h = pl.program_id(1)
    qi = pl.program_id(2)
    slope = slopes_ref[A_HEADS + h]
    q = q_ref[...]
    lo = lax.broadcasted_iota(jnp.int32, (TQ, LANES), 1) < HEAD_DIM
    zero = jnp.zeros_like(q)
    q2 = jnp.concatenate([jnp.where(lo, q, zero), jnp.where(lo, zero, q)], axis=0)
    rel = (lax.broadcasted_iota(jnp.int32, (2 * TQ, TQ), 0) % TQ
           - lax.broadcasted_iota(jnp.int32, (2 * TQ, TQ), 1))
    rel_bias = -slope * rel.astype(F32)

    m_ref[...] = jnp.full(m_ref.shape, NEG_INF, F32)
    l_ref[...] = jnp.zeros(l_ref.shape, F32)
    acc_ref[...] = jnp.zeros(acc_ref.shape, F32)

    def step(j, masked):
        ks = pl.ds(pl.multiple_of(j * TQ, TQ), TQ)
        k = k_ref[ks, :]
        v = v_ref[ks, :]
        base = slope * ((qi - j) * TQ).astype(F32)
        s = _dot_nt(q2, k) + (rel_bias - base)
        if masked:
            s = jnp.where(rel >= 0, s, NEG_INF)
        m_old = m_ref[...]
        m_new = jnp.maximum(m_old, jnp.max(s, axis=1, keepdims=True))
        alpha = jnp.exp(m_old - m_new)
        e = jnp.exp(s - m_new)
        l_ref[...] = alpha * l_ref[...] + jnp.sum(e, axis=1, keepdims=True)
        acc_ref[...] = alpha * acc_ref[...] + _dot(e.astype(BF16), v)
        m_ref[...] = m_new

    def body(j, carry):
        step(j, False)
        return carry
    lax.fori_loop(0, qi, body, 0)
    step(qi, True)

    lq = lam_ref[...]
    lam = (jnp.exp(jnp.sum(lq[0:1] * lq[1:2], axis=1, keepdims=True))
           - jnp.exp(jnp.sum(lq[2:3] * lq[3:4], axis=1, keepdims=True)) + lam_init)
    on = acc_ref[...] / l_ref[...]
    o = on[:TQ] - lam * on[TQ:]
    o = _rms_rows(o, sg_ref[...]) * (1.0 - lam_init)
    o_ref[...] = o.astype(o_ref.dtype)


def _diff_attention(proj3, slopes, lam_vecs, sub_gain, lam_init):
    B, S, _ = proj3.shape
    qoff = 3 * A_HEADS * HEAD_DIM // LANES
    koff = qoff + B_HEADS
    voff = koff + B_HEADS
    return pl.pallas_call(
        functools.partial(_diff_kernel, lam_init=lam_init),
        grid=(B, B_HEADS, S // TQ),
        in_specs=[
            pl.BlockSpec(memory_space=pltpu.SMEM),
            pl.BlockSpec((None, TQ, LANES), lambda b, h, i: (b, i, qoff + h)),
            pl.BlockSpec((None, S, LANES), lambda b, h, i: (b, 0, koff + h)),
            pl.BlockSpec((None, S, LANES), lambda b, h, i: (b, 0, voff + h)),
            pl.BlockSpec((4, HEAD_DIM), lambda b, h, i: (0, 0)),
            pl.BlockSpec((1, 2 * HEAD_DIM), lambda b, h, i: (0, 0)),
        ],
        out_specs=pl.BlockSpec((None, TQ, LANES), lambda b, h, i: (b, i, h)),
        out_shape=jax.ShapeDtypeStruct((B, S, B_HEADS * 2 * HEAD_DIM), BF16),
        scratch_shapes=[
            pltpu.VMEM((2 * TQ, 1), F32),
            pltpu.VMEM((2 * TQ, 1), F32),
            pltpu.VMEM((2 * TQ, LANES), F32),
        ],
        compiler_params=_params(("parallel", "parallel", "parallel")),
        name="diff_attn",
    )(slopes, proj3, proj3, proj3, lam_vecs, sub_gain)


def _outproj_kernel(h_ref, a_ref, b_ref, w_ref, o_ref):
    na = a_ref.shape[1]
    y = _dot(a_ref[...], w_ref[0:na, :]) + _dot(b_ref[...], w_ref[na:, :])
    o_ref[...] = h_ref[...] + y


def _attn_outproj(h2, oa2, ob2, w, tm=512):
    T, D = h2.shape
    return pl.pallas_call(
        _outproj_kernel,
        grid=(T // tm,),
        in_specs=[
            pl.BlockSpec((tm, D), lambda i: (i, 0)),
            pl.BlockSpec((tm, oa2.shape[1]), lambda i: (i, 0)),
            pl.BlockSpec((tm, ob2.shape[1]), lambda i: (i, 0)),
            pl.BlockSpec(w.shape, lambda i: (0, 0)),
        ],
        out_specs=pl.BlockSpec((tm, D), lambda i: (i, 0)),
        out_shape=jax.ShapeDtypeStruct((T, D), F32),
        compiler_params=_params(("parallel",)),
        name="attn_outproj",
    )(h2, oa2, ob2, w)


FFN_CHUNK = 512
HALO = SUBLANES


def _causal_conv_from(buf, w_ref, b_ref, cols, tm, taps):
    out = b_ref[:, cols]
    for kk in range(taps):
        off = HALO - (taps - 1) + kk
        out = out + w_ref[kk:kk + 1, cols] * buf[off:off + tm, :]
    return out


def _ffn_kernel(h_ref, g_ref, wup_ref, cw_ref, cb_ref, wdn_ref, o_ref,
                gbuf, vbuf, carry, *, taps):
    tm = h_ref.shape[0]
    F = wdn_ref.shape[0]
    first = pl.program_id(1) == 0

    @pl.when(first)
    def _():
        carry[...] = jnp.zeros(carry.shape, F32)

    h = h_ref[...]
    xn = _rms_rows(h, g_ref[...]).astype(BF16)
    acc = h
    for c in range(F // FFN_CHUNK):
        gc = slice(c * FFN_CHUNK, (c + 1) * FFN_CHUNK)
        vc = slice(F + c * FFN_CHUNK, F + (c + 1) * FFN_CHUNK)
        gbuf[0:HALO, :] = carry[:, gc]
        vbuf[0:HALO, :] = carry[:, vc]
        gbuf[HALO:, :] = _dot(xn, wup_ref[:, gc])
        vbuf[HALO:, :] = _dot(xn, wup_ref[:, vc])
        carry[:, gc] = gbuf[tm:tm + HALO, :]
        carry[:, vc] = vbuf[tm:tm + HALO, :]
        gg = _causal_conv_from(gbuf, cw_ref, cb_ref, gc, tm, taps)
        vv = _causal_conv_from(vbuf, cw_ref, cb_ref, vc, tm, taps)
        act = (_gelu(gg) * vv).astype(BF16)
        acc = acc + _dot(act, wdn_ref[gc, :])
    o_ref[...] = acc


def _conv_ffn(h3, g, wup, cw, cb, wdn, tm=256):
    B, S, D = h3.shape
    F2 = wup.shape[1]
    taps = cw.shape[0]
    const = lambda shape: pl.BlockSpec(shape, lambda b, i: (0,) * len(shape))
    return pl.pallas_call(
        functools.partial(_ffn_kernel, taps=taps),
        grid=(B, S // tm),
        in_specs=[
            pl.BlockSpec((None, tm, D), lambda b, i: (b, i, 0)),
            const((1, D)),
            const(wup.shape),
            const(cw.shape),
            const((1, F2)),
            const(wdn.shape),
        ],
        out_specs=pl.BlockSpec((None, tm, D), lambda b, i: (b, i, 0)),
        out_shape=jax.ShapeDtypeStruct((B, S, D), F32),
        scratch_shapes=[
            pltpu.VMEM((HALO + tm, FFN_CHUNK), F32),
            pltpu.VMEM((HALO + tm, FFN_CHUNK), F32),
            pltpu.VMEM((HALO, F2), F32),
        ],
        compiler_params=_params(("arbitrary", "arbitrary")),
        name="conv_ffn",
    )(h3, g, wup, cw, cb, wdn)


def _rec_kernel(h_ref, g_ref, win_ref, cw_ref, cb_ref, wa_ref, ba_ref, wx_ref, bx_ref,
                ap_ref, wout_ref, o_ref, xbuf, abuf, ubuf, hstate, *, taps):
    tm, C = h_ref.shape
    seg = tm // SUBLANES
    bw = C // LRU_BLOCKS
    first = pl.program_id(1) == 0

    @pl.when(first)
    def _():
        xbuf[0:HALO, :] = jnp.zeros((HALO, C), F32)
        hstate[...] = jnp.zeros(hstate.shape, F32)

    h = h_ref[...]
    xn = _rms_rows(h, g_ref[...]).astype(BF16)
    gate = _dot(xn, win_ref[:, 0:C])
    xbuf[HALO:, :] = _dot(xn, win_ref[:, C:])
    full = slice(0, C)
    xr = _causal_conv_from(xbuf, cw_ref, cb_ref, full, tm, taps)
    xbuf[0:HALO, :] = xbuf[tm:tm + HALO, :]

    ap = ap_ref[...]
    decay = -LRU_C * (jnp.maximum(-ap, 0.0) + jnp.log1p(jnp.exp(-jnp.abs(ap))))
    cins = []
    for n in range(LRU_BLOCKS):
        cols = slice(n * bw, (n + 1) * bw)
        xn_ = xr[:, cols]
        xb = xn_.astype(BF16)
        r = _sigmoid(_dot(xb, wa_ref[n]) + ba_ref[:, cols])
        i = _sigmoid(_dot(xb, wx_ref[n]) + bx_ref[:, cols])
        log_a = decay[:, cols] * r
        a = jnp.exp(log_a)
        abuf[n] = a
        ubuf[n] = jnp.sqrt(-jnp.tanh(log_a) * (1.0 + a * a)) * (i * xn_)

        hl = jnp.zeros((SUBLANES, bw), F32)
        pp = jnp.ones((SUBLANES, bw), F32)
        for j in range(seg):
            rows = pl.ds(j, SUBLANES, stride=seg)
            aj = abuf[n, rows, :]
            hl = aj * hl + ubuf[n, rows, :]
            pp = aj * pp
            ubuf[n, rows, :] = hl
            abuf[n, rows, :] = pp
        cin = hstate[0:1, cols]
        for s in range(SUBLANES):
            rows = slice(s * seg, (s + 1) * seg)
            ubuf[n, rows, :] = ubuf[n, rows, :] + abuf[n, rows, :] * cin
            cin = hl[s:s + 1, :] + pp[s:s + 1, :] * cin
        cins.append(cin)
    hstate[...] = jnp.broadcast_to(jnp.concatenate(cins, axis=1), hstate.shape)

    hs = jnp.concatenate([ubuf[n] for n in range(LRU_BLOCKS)], axis=1)
    y = (hs * _gelu(gate)).astype(BF16)
    o_ref[...] = h + _dot(y, wout_ref[...])


def _recurrent_block(h3, g, win, cw, cb, wa, ba, wx, bx, ap, wout, tm=256):
    B, S, D = h3.shape
    C = wout.shape[0]
    taps = cw.shape[0]
    const = lambda shape: pl.BlockSpec(shape, lambda b, i: (0,) * len(shape))
    return pl.pallas_call(
        functools.partial(_rec_kernel, taps=taps),
        grid=(B, S // tm),
        in_specs=[
            pl.BlockSpec((None, tm, D), lambda b, i: (b, i, 0)),
            const((1, D)), const(win.shape), const(cw.shape), const((1, C)),
            const(wa.shape), const((1, C)), const(wx.shape), const((1, C)),
            const((1, C)), const(wout.shape),
        ],
        out_specs=pl.BlockSpec((None, tm, D), lambda b, i: (b, i, 0)),
        out_shape=jax.ShapeDtypeStruct((B, S, D), F32),
        scratch_shapes=[
            pltpu.VMEM((HALO + tm, C), F32),
            pltpu.VMEM((LRU_BLOCKS, tm, C // LRU_BLOCKS), F32),
            pltpu.VMEM((LRU_BLOCKS, tm, C // LRU_BLOCKS), F32),
            pltpu.VMEM((SUBLANES, C), F32),
        ],
        compiler_params=_params(("arbitrary", "arbitrary")),
        name="recurrent_block",
    )(h3, g, win, cw, cb, wa, ba, wx, bx, ap, wout)


def _alibi_slopes(n):
    return jnp.exp2(-8.0 * jnp.arange(1, n + 1, dtype=F32) / n)


def _row(v):
    return v.reshape(1, -1).astype(F32)


def kernel(x, attn_norm, attn_w_in, attn_w_out, a_q_norm, a_k_norm, b_q_norm, b_k_norm, b_sub_norm,
           b_lam_q1, b_lam_k1, b_lam_q2, b_lam_k2, rec_norm, rec_w_in, rec_conv_w, rec_conv_b,
           rec_gate_a_w, rec_gate_a_b, rec_gate_x_w, rec_gate_x_b, rec_a_param, rec_w_out,
           ffn_norm, ffn_w_up, ffn_conv_w, ffn_conv_b, ffn_w_down):
    B, S, D = x.shape
    depth = ffn_norm.shape[0]
    slopes = _alibi_slopes(A_HEADS + B_HEADS)
    h = x
    for layer in range(depth):
        j = layer // 2
        if layer % 2 == 0:
            lam_init = 0.8 - 0.6 * math.exp(-0.3 * layer)
            scale = HEAD_DIM ** -0.5
            reps = SEC // HEAD_DIM
            ones = jnp.ones((SEC,), F32)
            head_gains = jnp.stack([
                jnp.tile(a_q_norm[j].astype(F32), reps) * scale,
                jnp.tile(a_k_norm[j].astype(F32), reps), ones,
                jnp.tile(b_q_norm[j].astype(F32), reps) * scale,
                jnp.tile(b_k_norm[j].astype(F32), reps), ones])
            proj = _attn_inproj(h.reshape(B * S, D), _row(attn_norm[j]),
                                attn_w_in[j].astype(BF16), head_gains)
            proj3 = proj.reshape(B, S, -1)
            oa = _dilated_attention(proj3, slopes)
            lam_vecs = jnp.stack([b_lam_q1[j], b_lam_k1[j], b_lam_q2[j], b_lam_k2[j]]).astype(F32)
            ob = _diff_attention(proj3, slopes, lam_vecs, _row(b_sub_norm[j]), lam_init)
            h = _attn_outproj(h.reshape(B * S, D), oa.reshape(B * S, -1), ob.reshape(B * S, -1),
                              attn_w_out[j].astype(BF16)).reshape(B, S, D)
        else:
            h = _recurrent_block(
                h, _row(rec_norm[j]), rec_w_in[j].astype(BF16), rec_conv_w[j].astype(F32),
                _row(rec_conv_b[j]), rec_gate_a_w[j].astype(BF16), _row(rec_gate_a_b[j]),
                rec_gate_x_w[j].astype(BF16), _row(rec_gate_x_b[j]), _row(rec_a_param[j]),
                rec_w_out[j].astype(BF16))
        h = _conv_ffn(h, _row(ffn_norm[layer]), ffn_w_up[layer].astype(BF16),
                      ffn_conv_w[layer].astype(F32), _row(ffn_conv_b[layer]),
                      ffn_w_down[layer].astype(BF16))
    return h
```

```python
import functools
import math

import numpy as np
import jax
import jax.numpy as jnp
from jax import lax
from jax.experimental import pallas as pl
from jax.experimental.pallas import tpu as pltpu

F32 = jnp.float32
BF16 = jnp.bfloat16

HEAD_DIM = 64
A_HEADS = 8
B_HEADS = 4
DILATED_CONFIGS = ((128, 1), (512, 4), (2048, 16))
BAND = 128
MAX_WINDOW = 2048
LRU_BLOCKS = 8
LRU_C = 8.0
NORM_EPS = 1e-6
NEG_INF = -1e30
LOG2E = math.log2(math.e)
LANES = 128
SUBLANES = 8
VMEM_LIMIT = 56 * 1024 * 1024


def _gelu(x):
    c = math.sqrt(2.0 / math.pi)
    return x * (0.5 * (1.0 + jnp.tanh(c * (x + 0.044715 * (x * x * x)))))


def _sigmoid(x):
    return 1.0 / (1.0 + jnp.exp(-x))


def _rms_rows(x, g):
    ms = jnp.mean(x * x, axis=-1, keepdims=True)
    return x * lax.rsqrt(ms + NORM_EPS) * g


def _dot(a, b):
    return jnp.dot(a, b, preferred_element_type=F32)


def _dot_nt(a, b):
    return lax.dot_general(a, b, (((1,), (1,)), ((), ())), preferred_element_type=F32)


def _params(sem):
    return pltpu.CompilerParams(dimension_semantics=sem, vmem_limit_bytes=VMEM_LIMIT)


SEC = 512
N_SEC = 6


def _inproj_kernel(x_ref, g_ref, w_ref, hg_ref, p_ref, o_ref):
    xn = _rms_rows(x_ref[...], g_ref[...]).astype(BF16)
    for s in range(N_SEC):
        y = _dot(xn, w_ref[:, s * SEC:(s + 1) * SEC])
        if s % 3 == 2:
            o_ref[:, s * SEC:(s + 1) * SEC] = y.astype(BF16)
        else:
            ms = _dot((y * y).astype(BF16), p_ref[...])
            o_ref[:, s * SEC:(s + 1) * SEC] = (
                y * lax.rsqrt(ms + NORM_EPS) * hg_ref[s:s + 1, :]).astype(BF16)


def _attn_inproj(x2, g, w, head_gains, tm=512):
    T, D = x2.shape
    N = w.shape[1]
    blk = np.kron(np.eye(SEC // HEAD_DIM), np.full((HEAD_DIM, HEAD_DIM), 1.0 / HEAD_DIM))
    pmat = jnp.asarray(blk, dtype=BF16)
    return pl.pallas_call(
        _inproj_kernel,
        grid=(T // tm,),
        in_specs=[
            pl.BlockSpec((tm, D), lambda i: (i, 0)),
            pl.BlockSpec((1, D), lambda i: (0, 0)),
            pl.BlockSpec((D, N), lambda i: (0, 0)),
            pl.BlockSpec((N_SEC, SEC), lambda i: (0, 0)),
            pl.BlockSpec((SEC, SEC), lambda i: (0, 0)),
        ],
        out_specs=pl.BlockSpec((tm, N), lambda i: (i, 0)),
        out_shape=jax.ShapeDtypeStruct((T, N), BF16),
        compiler_params=_params(("parallel",)),
        name="attn_inproj",
    )(x2, g, w, head_gains, pmat)


def _dilated_kernel(slopes_ref, q_ref, k_ref, v_ref, o_ref, qf, kf, vf, ob, lb):
    S = q_ref.shape[0]
    pad = kf.shape[0] - S
    g = pl.program_id(1)
    qf[...] = q_ref[...].astype(F32)
    kf[0:pad, :] = jnp.zeros((pad, LANES), F32)
    vf[0:pad, :] = jnp.zeros((pad, LANES), F32)
    kf[pad:, :] = k_ref[...].astype(F32)
    vf[pad:, :] = v_ref[...].astype(F32)

    lo = lax.broadcasted_iota(jnp.int32, (BAND, LANES), 1) < HEAD_DIM
    ii = lax.broadcasted_iota(jnp.int32, (BAND, 2 * BAND), 0)
    jj = lax.broadcasted_iota(jnp.int32, (BAND, 2 * BAND), 1)
    delta = ii + BAND - jj
    in_band = (delta >= 0) & (delta <= BAND)
    prev_half = lax.broadcasted_iota(jnp.int32, (2 * BAND, 2 * BAND), 1) < BAND
    sl0 = slopes_ref[2 * g]
    sl1 = slopes_ref[2 * g + 1]

    for bi, (window, d) in enumerate(DILATED_CONFIGS):
        span = BAND * d
        nbs = S // span
        dist = (delta * d).astype(F32)
        bias = jnp.concatenate(
            [jnp.where(in_band, -sl0 * dist, NEG_INF),
             jnp.where(in_band, -sl1 * dist, NEG_INF)], axis=0)

        def body(blk, carry, d=d, span=span, nbs=nbs, bias=bias, bi=bi):
            r = blk // nbs
            nb = blk - r * nbs
            start = nb * span + r
            q = qf[pl.ds(start, BAND, stride=d), :].astype(BF16)
            k = kf[pl.ds(pad + start - span, 2 * BAND, stride=d), :].astype(BF16)
            v = vf[pl.ds(pad + start - span, 2 * BAND, stride=d), :].astype(BF16)
            zero = jnp.zeros_like(q)
            q2 = jnp.concatenate([jnp.where(lo, q, zero), jnp.where(lo, zero, q)], axis=0)
            s = _dot_nt(q2, k) + bias
            s = jnp.where((nb == 0) & prev_half, NEG_INF, s)
            m = jnp.max(s, axis=1, keepdims=True)
            e = jnp.exp2(s - m)
            den = jnp.sum(e, axis=1, keepdims=True)
            o2 = _dot(e.astype(BF16), v) / den
            lse = m + jnp.log2(den)
            o = jnp.where(lo, o2[:BAND], o2[BAND:])
            l = jnp.where(lo, jnp.broadcast_to(lse[:BAND], (BAND, LANES)),
                          jnp.broadcast_to(lse[BAND:], (BAND, LANES)))
            ob[bi, pl.ds(start, BAND, stride=d), :] = o
            lb[bi, pl.ds(start, BAND, stride=d), :] = l
            return carry

        lax.fori_loop(0, S // BAND, body, 0, unroll=4)

    rows = 512
    def mix(c, carry):
        sl = pl.ds(pl.multiple_of(c * rows, rows), rows)
        l0, l1, l2 = lb[0, sl, :], lb[1, sl, :], lb[2, sl, :]
        m = jnp.maximum(jnp.maximum(l0, l1), l2)
        w0, w1, w2 = jnp.exp2(l0 - m), jnp.exp2(l1 - m), jnp.exp2(l2 - m)
        tot = w0 + w1 + w2
        o = (w0 * ob[0, sl, :] + w1 * ob[1, sl, :] + w2 * ob[2, sl, :]) / tot
        o_ref[sl, :] = o.astype(o_ref.dtype)
        return carry
    lax.fori_loop(0, S // rows, mix, 0)


def _dilated_attention(proj3, slopes):
    B, S, _ = proj3.shape
    assert S % MAX_WINDOW == 0
    pairs = A_HEADS // 2
    blk = lambda off: pl.BlockSpec((None, S, LANES), lambda b, g, off=off: (b, 0, off + g))
    return pl.pallas_call(
        _dilated_kernel,
        grid=(B, pairs),
        in_specs=[
            pl.BlockSpec(memory_space=pltpu.SMEM),
            blk(0), blk(pairs), blk(2 * pairs),
        ],
        out_specs=pl.BlockSpec((None, S, LANES), lambda b, g: (b, 0, g)),
        out_shape=jax.ShapeDtypeStruct((B, S, A_HEADS * HEAD_DIM), BF16),
        scratch_shapes=[
            pltpu.VMEM((S, LANES), F32),
            pltpu.VMEM((MAX_WINDOW + S, LANES), F32),
            pltpu.VMEM((MAX_WINDOW + S, LANES), F32),
            pltpu.VMEM((3, S, LANES), F32),
            pltpu.VMEM((3, S, LANES), F32),
        ],
        compiler_params=_params(("parallel", "parallel")),
        name="dilated_attn",
    )(slopes, proj3, proj3, proj3)


TQ = 256


def _diff_kernel(slopes_ref, q_ref, k_ref, v_ref, lam_ref, sg_ref, o_ref,
                 m_ref, l_ref, acc_ref, bias_ref, *, lam_init):
    h = pl.program_id(1)
    qi = pl.program_id(2)
    slope = slopes_ref[h]
    q = q_ref[...]
    lo = lax.broadcasted_iota(jnp.int32, (TQ, LANES), 1) < HEAD_DIM
    zero = jnp.zeros_like(q)
    q2 = jnp.concatenate([jnp.where(lo, q, zero), jnp.where(lo, zero, q)], axis=0)
    rel = (lax.broadcasted_iota(jnp.int32, (2 * TQ, TQ), 0) % TQ
           - lax.broadcasted_iota(jnp.int32, (2 * TQ, TQ), 1))
    bias_ref[...] = -slope * rel.astype(F32)

    m_ref[...] = jnp.full(m_ref.shape, NEG_INF, F32)
    l_ref[...] = jnp.zeros(l_ref.shape, F32)
    acc_ref[...] = jnp.zeros(acc_ref.shape, F32)

    def step(j, masked):
        ks = pl.ds(pl.multiple_of(j * TQ, TQ), TQ)
        k = k_ref[ks, :]
        v = v_ref[ks, :]
        base = slope * ((qi - j) * TQ).astype(F32)
        s = _dot_nt(q2, k) + bias_ref[...]
        if masked:
            s = jnp.where(rel >= 0, s, NEG_INF)
        m_old = m_ref[...]
        m_new = jnp.maximum(m_old, jnp.max(s, axis=1, keepdims=True) - base)
        alpha = jnp.exp2(m_old - m_new)
        shift = m_new + base
        e = jnp.exp2(s - jnp.concatenate([shift] * (TQ // LANES), axis=1))
        l_ref[...] = alpha * l_ref[...] + jnp.sum(e, axis=1, keepdims=True)
        acc_ref[...] = alpha * acc_ref[...] + _dot(e.astype(BF16), v)
        m_ref[...] = m_new

    def body(jp, carry):
        step(2 * jp, False)
        step(2 * jp + 1, False)
        return carry
    lax.fori_loop(0, qi // 2, body, 0)

    @pl.when(qi % 2 == 1)
    def _():
        step(qi - 1, False)
    step(qi, True)

    lq = lam_ref[...]
    lam = (jnp.exp(jnp.sum(lq[0:1] * lq[1:2], axis=1, keepdims=True))
           - jnp.exp(jnp.sum(lq[2:3] * lq[3:4], axis=1, keepdims=True)) + lam_init)
    on = acc_ref[...] / l_ref[...]
    o = on[:TQ] - lam * on[TQ:]
    o = _rms_rows(o, sg_ref[...]) * (1.0 - lam_init)
    o_ref[...] = o.astype(o_ref.dtype)


def _diff_attention(proj3, slopes, lam_vecs, sub_gain, lam_init):
    B, S, _ = proj3.shape
    qoff = 3 * A_HEADS * HEAD_DIM // LANES
    koff = qoff + B_HEADS
    voff = koff + B_HEADS
    return pl.pallas_call(
        functools.partial(_diff_kernel, lam_init=lam_init),
        grid=(B, B_HEADS, S // TQ),
        in_specs=[
            pl.BlockSpec(memory_space=pltpu.SMEM),
            pl.BlockSpec((None, TQ, LANES), lambda b, h, i: (b, i, qoff + h)),
            pl.BlockSpec((None, S, LANES), lambda b, h, i: (b, 0, koff + h)),
            pl.BlockSpec((None, S, LANES), lambda b, h, i: (b, 0, voff + h)),
            pl.BlockSpec((4, HEAD_DIM), lambda b, h, i: (0, 0)),
            pl.BlockSpec((1, 2 * HEAD_DIM), lambda b, h, i: (0, 0)),
        ],
        out_specs=pl.BlockSpec((None, TQ, LANES), lambda b, h, i: (b, i, h)),
        out_shape=jax.ShapeDtypeStruct((B, S, B_HEADS * 2 * HEAD_DIM), BF16),
        scratch_shapes=[
            pltpu.VMEM((2 * TQ, LANES), F32),
            pltpu.VMEM((2 * TQ, LANES), F32),
            pltpu.VMEM((2 * TQ, LANES), F32),
            pltpu.VMEM((2 * TQ, TQ), F32),
        ],
        compiler_params=_params(("parallel", "parallel", "parallel")),
        name="diff_attn",
    )(slopes, proj3, proj3, proj3, lam_vecs, sub_gain)


def _outproj_kernel(h_ref, a_ref, b_ref, w_ref, o_ref):
    na = a_ref.shape[1]
    y = _dot(a_ref[...], w_ref[0:na, :]) + _dot(b_ref[...], w_ref[na:, :])
    o_ref[...] = h_ref[...] + y


def _attn_outproj(h2, oa2, ob2, w, tm=512):
    T, D = h2.shape
    return pl.pallas_call(
        _outproj_kernel,
        grid=(T // tm,),
        in_specs=[
            pl.BlockSpec((tm, D), lambda i: (i, 0)),
            pl.BlockSpec((tm, oa2.shape[1]), lambda i: (i, 0)),
            pl.BlockSpec((tm, ob2.shape[1]), lambda i: (i, 0)),
            pl.BlockSpec(w.shape, lambda i: (0, 0)),
        ],
        out_specs=pl.BlockSpec((tm, D), lambda i: (i, 0)),
        out_shape=jax.ShapeDtypeStruct((T, D), F32),
        compiler_params=_params(("parallel",)),
        name="attn_outproj",
    )(h2, oa2, ob2, w)


FFN_CHUNK = 512
HALO = SUBLANES


def _causal_conv_from(buf, w_ref, b_ref, cols, tm, taps):
    out = b_ref[:, cols]
    for kk in range(taps):
        off = HALO - (taps - 1) + kk
        out = out + w_ref[kk:kk + 1, cols] * buf[off:off + tm, :]
    return out


def _ffn_kernel(h_ref, g_ref, wup_ref, cw_ref, cb_ref, wdn_ref, o_ref,
                gbuf, vbuf, carry, *, taps):
    tm = h_ref.shape[0]
    F = wdn_ref.shape[0]
    first = pl.program_id(1) == 0

    @pl.when(first)
    def _():
        carry[...] = jnp.zeros(carry.shape, F32)

    h = h_ref[...]
    xn = _rms_rows(h, g_ref[...]).astype(BF16)
    acc = h
    for c in range(F // FFN_CHUNK):
        gc = slice(c * FFN_CHUNK, (c + 1) * FFN_CHUNK)
        vc = slice(F + c * FFN_CHUNK, F + (c + 1) * FFN_CHUNK)
        gbuf[0:HALO, :] = carry[:, gc]
        vbuf[0:HALO, :] = carry[:, vc]
        gbuf[HALO:, :] = _dot(xn, wup_ref[:, gc])
        vbuf[HALO:, :] = _dot(xn, wup_ref[:, vc])
        carry[:, gc] = gbuf[tm:tm + HALO, :]
        carry[:, vc] = vbuf[tm:tm + HALO, :]
        gg = _causal_conv_from(gbuf, cw_ref, cb_ref, gc, tm, taps)
        vv = _causal_conv_from(vbuf, cw_ref, cb_ref, vc, tm, taps)
        act = (_gelu(gg) * vv).astype(BF16)
        acc = acc + _dot(act, wdn_ref[gc, :])
    o_ref[...] = acc


def _conv_ffn(h3, g, wup, cw, cb, wdn, tm=256):
    B, S, D = h3.shape
    F2 = wup.shape[1]
    taps = cw.shape[0]
    const = lambda shape: pl.BlockSpec(shape, lambda b, i: (0,) * len(shape))
    return pl.pallas_call(
        functools.partial(_ffn_kernel, taps=taps),
        grid=(B, S // tm),
        in_specs=[
            pl.BlockSpec((None, tm, D), lambda b, i: (b, i, 0)),
            const((1, D)),
            const(wup.shape),
            const(cw.shape),
            const((1, F2)),
            const(wdn.shape),
        ],
        out_specs=pl.BlockSpec((None, tm, D), lambda b, i: (b, i, 0)),
        out_shape=jax.ShapeDtypeStruct((B, S, D), F32),
        scratch_shapes=[
            pltpu.VMEM((HALO + tm, FFN_CHUNK), F32),
            pltpu.VMEM((HALO + tm, FFN_CHUNK), F32),
            pltpu.VMEM((HALO, F2), F32),
        ],
        compiler_params=_params(("arbitrary", "arbitrary")),
        name="conv_ffn",
    )(h3, g, wup, cw, cb, wdn)


def _rec_kernel(h_ref, g_ref, win_ref, cw_ref, cb_ref, wa_ref, ba_ref, wx_ref, bx_ref,
                ap_ref, wout_ref, o_ref, xbuf, abuf, ubuf, hstate, *, taps):
    tm, C = h_ref.shape
    seg = tm // SUBLANES
    bw = C // LRU_BLOCKS
    first = pl.program_id(1) == 0

    @pl.when(first)
    def _():
        xbuf[0:HALO, :] = jnp.zeros((HALO, C), F32)
        hstate[...] = jnp.zeros(hstate.shape, F32)

    h = h_ref[...]
    xn = _rms_rows(h, g_ref[...]).astype(BF16)
    gate = _dot(xn, win_ref[:, 0:C])
    xbuf[HALO:, :] = _dot(xn, win_ref[:, C:])
    full = slice(0, C)
    xr = _causal_conv_from(xbuf, cw_ref, cb_ref, full, tm, taps)
    xbuf[0:HALO, :] = xbuf[tm:tm + HALO, :]

    ap = ap_ref[...]
    decay = -LRU_C * (jnp.maximum(-ap, 0.0) + jnp.log1p(jnp.exp(-jnp.abs(ap))))
    cins = []
    for n in range(LRU_BLOCKS):
        cols = slice(n * bw, (n + 1) * bw)
        xn_ = xr[:, cols]
        xb = xn_.astype(BF16)
        r = _sigmoid(_dot(xb, wa_ref[n]) + ba_ref[:, cols])
        i = _sigmoid(_dot(xb, wx_ref[n]) + bx_ref[:, cols])
        log_a = decay[:, cols] * r
        a = jnp.exp(log_a)
        abuf[n] = a
        ubuf[n] = jnp.sqrt(-jnp.tanh(log_a) * (1.0 + a * a)) * (i * xn_)

        hl = jnp.zeros((SUBLANES, bw), F32)
        pp = jnp.ones((SUBLANES, bw), F32)
        for j in range(seg):
            rows = pl.ds(j, SUBLANES, stride=seg)
            aj = abuf[n, rows, :]
            hl = aj * hl + ubuf[n, rows, :]
            pp = aj * pp
            ubuf[n, rows, :] = hl
            abuf[n, rows, :] = pp
        cin = hstate[0:1, cols]
        for s in range(SUBLANES):
            rows = slice(s * seg, (s + 1) * seg)
            ubuf[n, rows, :] = ubuf[n, rows, :] + abuf[n, rows, :] * cin
            cin = hl[s:s + 1, :] + pp[s:s + 1, :] * cin
        cins.append(cin)
    hstate[...] = jnp.broadcast_to(jnp.concatenate(cins, axis=1), hstate.shape)

    hs = jnp.concatenate([ubuf[n] for n in range(LRU_BLOCKS)], axis=1)
    y = (hs * _gelu(gate)).astype(BF16)
    o_ref[...] = h + _dot(y, wout_ref[...])


def _recurrent_block(h3, g, win, cw, cb, wa, ba, wx, bx, ap, wout, tm=256):
    B, S, D = h3.shape
    C = wout.shape[0]
    taps = cw.shape[0]
    const = lambda shape: pl.BlockSpec(shape, lambda b, i: (0,) * len(shape))
    return pl.pallas_call(
        functools.partial(_rec_kernel, taps=taps),
        grid=(B, S // tm),
        in_specs=[
            pl.BlockSpec((None, tm, D), lambda b, i: (b, i, 0)),
            const((1, D)), const(win.shape), const(cw.shape), const((1, C)),
            const(wa.shape), const((1, C)), const(wx.shape), const((1, C)),
            const((1, C)), const(wout.shape),
        ],
        out_specs=pl.BlockSpec((None, tm, D), lambda b, i: (b, i, 0)),
        out_shape=jax.ShapeDtypeStruct((B, S, D), F32),
        scratch_shapes=[
            pltpu.VMEM((HALO + tm, C), F32),
            pltpu.VMEM((LRU_BLOCKS, tm, C // LRU_BLOCKS), F32),
            pltpu.VMEM((LRU_BLOCKS, tm, C // LRU_BLOCKS), F32),
            pltpu.VMEM((SUBLANES, C), F32),
        ],
        compiler_params=_params(("arbitrary", "arbitrary")),
        name="recurrent_block",
    )(h3, g, win, cw, cb, wa, ba, wx, bx, ap, wout)


def _alibi_slopes(n):
    return jnp.exp2(-8.0 * jnp.arange(1, n + 1, dtype=F32) / n)


def _row(v):
    return v.reshape(1, -1).astype(F32)


def kernel(x, attn_norm, attn_w_in, attn_w_out, a_q_norm, a_k_norm, b_q_norm, b_k_norm, b_sub_norm,
           b_lam_q1, b_lam_k1, b_lam_q2, b_lam_k2, rec_norm, rec_w_in, rec_conv_w, rec_conv_b,
           rec_gate_a_w, rec_gate_a_b, rec_gate_x_w, rec_gate_x_b, rec_a_param, rec_w_out,
           ffn_norm, ffn_w_up, ffn_conv_w, ffn_conv_b, ffn_w_down):
    B, S, D = x.shape
    depth = ffn_norm.shape[0]
    slopes = _alibi_slopes(A_HEADS + B_HEADS)
    h = x
    for layer in range(depth):
        j = layer // 2
        if layer % 2 == 0:
            lam_init = 0.8 - 0.6 * math.exp(-0.3 * layer)
            scale = HEAD_DIM ** -0.5
            reps = SEC // HEAD_DIM
            ones = jnp.ones((SEC,), F32)
            head_gains = jnp.stack([
                jnp.tile(a_q_norm[j].astype(F32), reps) * (scale * LOG2E),
                jnp.tile(a_k_norm[j].astype(F32), reps), ones,
                jnp.tile(b_q_norm[j].astype(F32), reps) * (scale * LOG2E),
                jnp.tile(b_k_norm[j].astype(F32), reps), ones])
            proj = _attn_inproj(h.reshape(B * S, D), _row(attn_norm[j]),
                                attn_w_in[j].astype(BF16), head_gains)
            proj3 = proj.reshape(B, S, -1)
            oa = _dilated_attention(proj3, slopes[:A_HEADS] * LOG2E)
            lam_vecs = jnp.stack([b_lam_q1[j], b_lam_k1[j], b_lam_q2[j], b_lam_k2[j]]).astype(F32)
            ob = _diff_attention(proj3, slopes[A_HEADS:] * LOG2E, lam_vecs, _row(b_sub_norm[j]),
                                 lam_init)
            h = _attn_outproj(h.reshape(B * S, D), oa.reshape(B * S, -1), ob.reshape(B * S, -1),
                              attn_w_out[j].astype(BF16)).reshape(B, S, D)
        else:
            h = _recurrent_block(
                h, _row(rec_norm[j]), rec_w_in[j].astype(BF16), rec_conv_w[j].astype(F32),
                _row(rec_conv_b[j]), rec_gate_a_w[j].astype(BF16), _row(rec_gate_a_b[j]),
                rec_gate_x_w[j].astype(BF16), _row(rec_gate_x_b[j]), _row(rec_a_param[j]),
                rec_w_out[j].astype(BF16))
        h = _conv_ffn(h, _row(ffn_norm[layer]), ffn_w_up[layer].astype(BF16),
                      ffn_conv_w[layer].astype(F32), _row(ffn_conv_b[layer]),
                      ffn_w_down[layer].astype(BF16))
    return h
```

```python
import functools
import math

import numpy as np
import jax
import jax.numpy as jnp
from jax import lax
from jax.experimental import pallas as pl
from jax.experimental.pallas import tpu as pltpu

F32 = jnp.float32
BF16 = jnp.bfloat16

HEAD_DIM = 64
A_HEADS = 8
B_HEADS = 4
DILATED_CONFIGS = ((128, 1), (512, 4), (2048, 16))
BAND = 128
MAX_WINDOW = 2048
LRU_BLOCKS = 8
LRU_C = 8.0
NORM_EPS = 1e-6
NEG_INF = -1e30
LOG2E = math.log2(math.e)
LANES = 128
SUBLANES = 8
VMEM_LIMIT = 56 * 1024 * 1024


def _gelu(x):
    c = math.sqrt(2.0 / math.pi)
    return x * (0.5 * (1.0 + jnp.tanh(c * (x + 0.044715 * (x * x * x)))))


def _sigmoid(x):
    return 1.0 / (1.0 + jnp.exp(-x))


def _rms_rows(x, g):
    ms = jnp.mean(x * x, axis=-1, keepdims=True)
    return x * lax.rsqrt(ms + NORM_EPS) * g


def _dot(a, b):
    return jnp.dot(a, b, preferred_element_type=F32)


def _dot_nt(a, b):
    return lax.dot_general(a, b, (((1,), (1,)), ((), ())), preferred_element_type=F32)


def _params(sem):
    return pltpu.CompilerParams(dimension_semantics=sem, vmem_limit_bytes=VMEM_LIMIT)


SEC = 512
N_SEC = 6


def _inproj_kernel(x_ref, g_ref, w_ref, hg_ref, p_ref, o_ref):
    xn = _rms_rows(x_ref[...], g_ref[...]).astype(BF16)
    for s in range(N_SEC):
        y = _dot(xn, w_ref[:, s * SEC:(s + 1) * SEC])
        if s % 3 == 2:
            o_ref[:, s * SEC:(s + 1) * SEC] = y.astype(BF16)
        else:
            ms = _dot((y * y).astype(BF16), p_ref[...])
            o_ref[:, s * SEC:(s + 1) * SEC] = (
                y * lax.rsqrt(ms + NORM_EPS) * hg_ref[s:s + 1, :]).astype(BF16)


def _attn_inproj(x2, g, w, head_gains, tm=512):
    T, D = x2.shape
    N = w.shape[1]
    blk = np.kron(np.eye(SEC // HEAD_DIM), np.full((HEAD_DIM, HEAD_DIM), 1.0 / HEAD_DIM))
    pmat = jnp.asarray(blk, dtype=BF16)
    return pl.pallas_call(
        _inproj_kernel,
        grid=(T // tm,),
        in_specs=[
            pl.BlockSpec((tm, D), lambda i: (i, 0)),
            pl.BlockSpec((1, D), lambda i: (0, 0)),
            pl.BlockSpec((D, N), lambda i: (0, 0)),
            pl.BlockSpec((N_SEC, SEC), lambda i: (0, 0)),
            pl.BlockSpec((SEC, SEC), lambda i: (0, 0)),
        ],
        out_specs=pl.BlockSpec((tm, N), lambda i: (i, 0)),
        out_shape=jax.ShapeDtypeStruct((T, N), BF16),
        compiler_params=_params(("parallel",)),
        name="attn_inproj",
    )(x2, g, w, head_gains, pmat)


def _dilated_kernel(slopes_ref, q_ref, k_ref, v_ref, o_ref, qf, kf, vf, ob, lb, bias_buf):
    S = q_ref.shape[0]
    pad = kf.shape[0] - S
    g = pl.program_id(1)
    qf[...] = q_ref[...].astype(F32)
    kf[0:pad, :] = jnp.zeros((pad, LANES), F32)
    vf[0:pad, :] = jnp.zeros((pad, LANES), F32)
    kf[pad:, :] = k_ref[...].astype(F32)
    vf[pad:, :] = v_ref[...].astype(F32)

    lo = lax.broadcasted_iota(jnp.int32, (BAND, LANES), 1) < HEAD_DIM
    ii = lax.broadcasted_iota(jnp.int32, (BAND, 2 * BAND), 0)
    jj = lax.broadcasted_iota(jnp.int32, (BAND, 2 * BAND), 1)
    delta = ii + BAND - jj
    in_band = (delta >= 0) & (delta <= BAND)
    prev_half = lax.broadcasted_iota(jnp.int32, (2 * BAND, 2 * BAND), 1) < BAND
    sl0 = slopes_ref[2 * g]
    sl1 = slopes_ref[2 * g + 1]
    ones = jnp.ones((2 * BAND, LANES), BF16)

    for bi, (window, d) in enumerate(DILATED_CONFIGS):
        span = BAND * d
        nbs = S // span
        dist = (delta * d).astype(F32)
        bias = jnp.concatenate(
            [jnp.where(in_band, -sl0 * dist, NEG_INF),
             jnp.where(in_band, -sl1 * dist, NEG_INF)], axis=0)
        bias_buf[2 * bi] = bias
        bias_buf[2 * bi + 1] = jnp.where(prev_half, NEG_INF, bias)

        def body(blk, carry, d=d, span=span, nbs=nbs, bi=bi):
            r = blk // nbs
            nb = blk - r * nbs
            start = nb * span + r
            q = qf[pl.ds(start, BAND, stride=d), :].astype(BF16)
            k = kf[pl.ds(pad + start - span, 2 * BAND, stride=d), :].astype(BF16)
            v = vf[pl.ds(pad + start - span, 2 * BAND, stride=d), :].astype(BF16)
            zero = jnp.zeros_like(q)
            q2 = jnp.concatenate([jnp.where(lo, q, zero), jnp.where(lo, zero, q)], axis=0)
            s = _dot_nt(q2, k) + bias_buf[2 * bi + (nb == 0).astype(jnp.int32)]
            m = jnp.max(s, axis=1, keepdims=True)
            e = jnp.exp2(s - m)
            pv = _dot(e.astype(BF16), jnp.concatenate([v, ones], axis=1))
            den = pv[:, LANES:]
            o2 = pv[:, :LANES] / den
            lse = m + jnp.log2(den)
            ob[bi, pl.ds(start, BAND, stride=d), :] = jnp.where(lo, o2[:BAND], o2[BAND:])
            lb[bi, pl.ds(start, BAND, stride=d), :] = jnp.where(lo, lse[:BAND], lse[BAND:])
            return carry

        lax.fori_loop(0, S // BAND, body, 0, unroll=8)

    rows = 512
    def mix(c, carry):
        sl = pl.ds(pl.multiple_of(c * rows, rows), rows)
        l0, l1, l2 = lb[0, sl, :], lb[1, sl, :], lb[2, sl, :]
        m = jnp.maximum(jnp.maximum(l0, l1), l2)
        w0, w1, w2 = jnp.exp2(l0 - m), jnp.exp2(l1 - m), jnp.exp2(l2 - m)
        tot = w0 + w1 + w2
        o = (w0 * ob[0, sl, :] + w1 * ob[1, sl, :] + w2 * ob[2, sl, :]) / tot
        o_ref[sl, :] = o.astype(o_ref.dtype)
        return carry
    lax.fori_loop(0, S // rows, mix, 0)


def _dilated_attention(proj3, slopes):
    B, S, _ = proj3.shape
    assert S % MAX_WINDOW == 0
    pairs = A_HEADS // 2
    blk = lambda off: pl.BlockSpec((None, S, LANES), lambda b, g, off=off: (b, 0, off + g))
    return pl.pallas_call(
        _dilated_kernel,
        grid=(B, pairs),
        in_specs=[
            pl.BlockSpec(memory_space=pltpu.SMEM),
            blk(0), blk(pairs), blk(2 * pairs),
        ],
        out_specs=pl.BlockSpec((None, S, LANES), lambda b, g: (b, 0, g)),
        out_shape=jax.ShapeDtypeStruct((B, S, A_HEADS * HEAD_DIM), BF16),
        scratch_shapes=[
            pltpu.VMEM((S, LANES), F32),
            pltpu.VMEM((MAX_WINDOW + S, LANES), F32),
            pltpu.VMEM((MAX_WINDOW + S, LANES), F32),
            pltpu.VMEM((3, S, LANES), F32),
            pltpu.VMEM((3, S, LANES), F32),
            pltpu.VMEM((2 * len(DILATED_CONFIGS), 2 * BAND, 2 * BAND), F32),
        ],
        compiler_params=_params(("parallel", "parallel")),
        name="dilated_attn",
    )(slopes, proj3, proj3, proj3)


TQ = 256
KV_UNROLL = 4


def _diff_kernel(slopes_ref, q_ref, k_ref, v_ref, lam_ref, sg_ref, o_ref,
                 m_ref, acc_ref, *, lam_init):
    h = pl.program_id(1)
    qi = pl.program_id(2)
    slope = slopes_ref[h]
    q = q_ref[...]
    lo = lax.broadcasted_iota(jnp.int32, (TQ, LANES), 1) < HEAD_DIM
    zero = jnp.zeros_like(q)
    q2 = jnp.concatenate([jnp.where(lo, q, zero), jnp.where(lo, zero, q)], axis=0)
    kcol = slope * lax.broadcasted_iota(jnp.int32, (1, TQ), 1).astype(F32)
    ones = jnp.ones((TQ, LANES), BF16)

    m_ref[...] = jnp.full(m_ref.shape, NEG_INF, F32)
    acc_ref[...] = jnp.zeros(acc_ref.shape, F32)

    def step(j, masked):
        ks = pl.ds(pl.multiple_of(j * TQ, TQ), TQ)
        k = k_ref[ks, :]
        v1 = jnp.concatenate([v_ref[ks, :], ones], axis=1)
        s = _dot_nt(q2, k) + (kcol + slope * (j * TQ).astype(F32))
        if masked:
            rel = (lax.broadcasted_iota(jnp.int32, (2 * TQ, TQ), 0) % TQ
                   - lax.broadcasted_iota(jnp.int32, (2 * TQ, TQ), 1))
            s = jnp.where(rel >= 0, s, NEG_INF)
        m_old = m_ref[...]
        m_new = jnp.maximum(m_old, jnp.max(s, axis=1, keepdims=True))
        alpha = jnp.exp2(m_old - m_new)
        e = jnp.exp2(s - jnp.concatenate([m_new] * (TQ // LANES), axis=1))
        acc_ref[...] = (jnp.concatenate([alpha, alpha], axis=1) * acc_ref[...]
                        + _dot(e.astype(BF16), v1))
        m_ref[...] = m_new

    def body(jq, carry):
        for u in range(KV_UNROLL):
            step(KV_UNROLL * jq + u, False)
        return carry
    nq = qi // KV_UNROLL
    lax.fori_loop(0, nq, body, 0)

    rem = qi - nq * KV_UNROLL
    for r in range(KV_UNROLL):
        @pl.when(rem == r)
        def _(r=r):
            for u in range(r):
                step(qi - r + u, False)
            step(qi, True)

    lq = lam_ref[...]
    lam = (jnp.exp(jnp.sum(lq[0:1] * lq[1:2], axis=1, keepdims=True))
           - jnp.exp(jnp.sum(lq[2:3] * lq[3:4], axis=1, keepdims=True)) + lam_init)
    on = acc_ref[:, 0:LANES] / acc_ref[:, LANES:]
    o = on[:TQ] - lam * on[TQ:]
    o = _rms_rows(o, sg_ref[...]) * (1.0 - lam_init)
    o_ref[...] = o.astype(o_ref.dtype)


def _diff_attention(proj3, slopes, lam_vecs, sub_gain, lam_init):
    B, S, _ = proj3.shape
    qoff = 3 * A_HEADS * HEAD_DIM // LANES
    koff = qoff + B_HEADS
    voff = koff + B_HEADS
    return pl.pallas_call(
        functools.partial(_diff_kernel, lam_init=lam_init),
        grid=(B, B_HEADS, S // TQ),
        in_specs=[
            pl.BlockSpec(memory_space=pltpu.SMEM),
            pl.BlockSpec((None, TQ, LANES), lambda b, h, i: (b, i, qoff + h)),
            pl.BlockSpec((None, S, LANES), lambda b, h, i: (b, 0, koff + h)),
            pl.BlockSpec((None, S, LANES), lambda b, h, i: (b, 0, voff + h)),
            pl.BlockSpec((4, HEAD_DIM), lambda b, h, i: (0, 0)),
            pl.BlockSpec((1, 2 * HEAD_DIM), lambda b, h, i: (0, 0)),
        ],
        out_specs=pl.BlockSpec((None, TQ, LANES), lambda b, h, i: (b, i, h)),
        out_shape=jax.ShapeDtypeStruct((B, S, B_HEADS * 2 * HEAD_DIM), BF16),
        scratch_shapes=[
            pltpu.VMEM((2 * TQ, LANES), F32),
            pltpu.VMEM((2 * TQ, 2 * LANES), F32),
        ],
        compiler_params=_params(("parallel", "parallel", "parallel")),
        name="diff_attn",
    )(slopes, proj3, proj3, proj3, lam_vecs, sub_gain)


def _outproj_kernel(h_ref, a_ref, b_ref, w_ref, o_ref):
    na = a_ref.shape[1]
    y = _dot(a_ref[...], w_ref[0:na, :]) + _dot(b_ref[...], w_ref[na:, :])
    o_ref[...] = h_ref[...] + y


def _attn_outproj(h2, oa2, ob2, w, tm=512):
    T, D = h2.shape
    return pl.pallas_call(
        _outproj_kernel,
        grid=(T // tm,),
        in_specs=[
            pl.BlockSpec((tm, D), lambda i: (i, 0)),
            pl.BlockSpec((tm, oa2.shape[1]), lambda i: (i, 0)),
            pl.BlockSpec((tm, ob2.shape[1]), lambda i: (i, 0)),
            pl.BlockSpec(w.shape, lambda i: (0, 0)),
        ],
        out_specs=pl.BlockSpec((tm, D), lambda i: (i, 0)),
        out_shape=jax.ShapeDtypeStruct((T, D), F32),
        compiler_params=_params(("parallel",)),
        name="attn_outproj",
    )(h2, oa2, ob2, w)


FFN_CHUNK = 512


def _seg_pitch(seg):
    p = seg // SUBLANES + 1
    return SUBLANES * (p if p % 2 else p + 1)


def _to_segment_rows(src_ref, stage, dst_ref):
    tm, C = src_ref.shape
    seg = tm // SUBLANES
    pitch = _seg_pitch(seg)
    for n in range(C // LANES):
        cols = slice(n * LANES, (n + 1) * LANES)
        for s in range(SUBLANES):
            stage[n, s * pitch:s * pitch + seg, :] = src_ref[s * seg:(s + 1) * seg, cols]
    for j in range(seg):
        dst_ref[j * SUBLANES:(j + 1) * SUBLANES, :] = jnp.concatenate(
            [stage[n, pl.ds(j, SUBLANES, stride=pitch), :] for n in range(C // LANES)], axis=1)


def _from_segment_rows(val, stage, dst_ref):
    tm, C = val.shape
    seg = tm // SUBLANES
    pitch = _seg_pitch(seg)
    for n in range(C // LANES):
        cols = slice(n * LANES, (n + 1) * LANES)
        for j in range(seg):
            stage[n, pl.ds(j, SUBLANES, stride=pitch), :] = val[j * SUBLANES:(j + 1) * SUBLANES, cols]
        for s in range(SUBLANES):
            dst_ref[s * seg:(s + 1) * seg, cols] = stage[n, s * pitch:s * pitch + seg, :]


def _segment_conv(u, carry_ref, cols, w_ref, b_ref, taps):
    tm = u.shape[0]
    first_sublane = lax.broadcasted_iota(jnp.int32, (SUBLANES, u.shape[1]), 0) == 0
    wrapped = []
    for i in range(1, taps):
        tail = u[tm - i * SUBLANES:tm - (i - 1) * SUBLANES]
        wrapped.append(jnp.where(first_sublane,
                                 pltpu.roll(carry_ref[i - 1, :, cols], 1, 0),
                                 pltpu.roll(tail, 1, 0)))
        carry_ref[i - 1, :, cols] = tail
    out = b_ref[:, cols] + w_ref[taps - 1:taps, cols] * u
    for k in range(1, taps):
        shifted = jnp.concatenate(wrapped[k - 1::-1] + [u[:tm - k * SUBLANES]], axis=0)
        out = out + w_ref[taps - 1 - k:taps - k, cols] * shifted
    return out


def _ffn_kernel(h_ref, g_ref, wup_ref, cw_ref, cb_ref, wdn_ref, o_ref,
                stage, hseg, carry, *, taps):
    F = wdn_ref.shape[0]

    @pl.when(pl.program_id(1) == 0)
    def _():
        carry[...] = jnp.zeros(carry.shape, F32)

    _to_segment_rows(h_ref, stage, hseg)
    xn = _rms_rows(hseg[...], g_ref[...]).astype(BF16)
    acc = None
    for c in range(F // FFN_CHUNK):
        gc = slice(c * FFN_CHUNK, (c + 1) * FFN_CHUNK)
        vc = slice(F + c * FFN_CHUNK, F + (c + 1) * FFN_CHUNK)
        gg = _segment_conv(_dot(xn, wup_ref[:, gc]), carry, gc, cw_ref, cb_ref, taps)
        vv = _segment_conv(_dot(xn, wup_ref[:, vc]), carry, vc, cw_ref, cb_ref, taps)
        part = _dot((_gelu(gg) * vv).astype(BF16), wdn_ref[gc, :])
        acc = part if acc is None else acc + part
    _from_segment_rows(hseg[...] + acc, stage, o_ref)


def _stage_shape(tm, C):
    return (C // LANES, SUBLANES * _seg_pitch(tm // SUBLANES), LANES)


def _resident(shape):
    return pl.BlockSpec(shape, lambda b, i: (0,) * len(shape), pipeline_mode=pl.Buffered(1))


def _conv_ffn(h3, g, wup, cw, cb, wdn, tm=256):
    B, S, D = h3.shape
    F2 = wup.shape[1]
    taps = cw.shape[0]
    return pl.pallas_call(
        functools.partial(_ffn_kernel, taps=taps),
        grid=(B, S // tm),
        in_specs=[
            pl.BlockSpec((None, tm, D), lambda b, i: (b, i, 0)),
            _resident((1, D)),
            _resident(wup.shape),
            _resident(cw.shape),
            _resident((1, F2)),
            _resident(wdn.shape),
        ],
        out_specs=pl.BlockSpec((None, tm, D), lambda b, i: (b, i, 0)),
        out_shape=jax.ShapeDtypeStruct((B, S, D), F32),
        scratch_shapes=[
            pltpu.VMEM(_stage_shape(tm, D), F32),
            pltpu.VMEM((tm, D), F32),
            pltpu.VMEM((taps - 1, SUBLANES, F2), F32),
        ],
        compiler_params=_params(("arbitrary", "arbitrary")),
        name="conv_ffn",
    )(h3, g, wup, cw, cb, wdn)


def _rec_kernel(h_ref, g_ref, win_ref, cw_ref, cb_ref, wa_ref, ba_ref, wx_ref, bx_ref,
                ap_ref, wout_ref, o_ref, stage, hseg, carry, hstate, *, taps):
    tm, C = h_ref.shape
    seg = tm // SUBLANES
    bw = C // LRU_BLOCKS

    @pl.when(pl.program_id(1) == 0)
    def _():
        carry[...] = jnp.zeros(carry.shape, F32)
        hstate[...] = jnp.zeros(hstate.shape, F32)

    _to_segment_rows(h_ref, stage, hseg)
    xn = _rms_rows(hseg[...], g_ref[...]).astype(BF16)
    gate = _dot(xn, win_ref[:, 0:C])
    xr_all = _segment_conv(_dot(xn, win_ref[:, C:]), carry, slice(0, C), cw_ref, cb_ref, taps)

    ap = ap_ref[...]
    decay = -LRU_C * (jnp.maximum(-ap, 0.0) + jnp.log1p(jnp.exp(-jnp.abs(ap))))
    sublane = lax.broadcasted_iota(jnp.int32, (SUBLANES, bw), 0)
    blocks = []
    for n in range(LRU_BLOCKS):
        cols = slice(n * bw, (n + 1) * bw)
        xr = xr_all[:, cols]
        xb = xr.astype(BF16)
        r = _sigmoid(_dot(xb, wa_ref[n]) + ba_ref[:, cols])
        i = _sigmoid(_dot(xb, wx_ref[n]) + bx_ref[:, cols])
        log_a = decay[:, cols] * r
        a = jnp.exp(log_a)
        u = jnp.sqrt(-jnp.tanh(log_a) * (1.0 + a * a)) * (i * xr)

        hl = jnp.zeros((SUBLANES, bw), F32)
        pp = jnp.ones((SUBLANES, bw), F32)
        hls, pps = [], []
        for j in range(seg):
            aj = a[j * SUBLANES:(j + 1) * SUBLANES]
            hl = aj * hl + u[j * SUBLANES:(j + 1) * SUBLANES]
            pp = aj * pp
            hls.append(hl)
            pps.append(pp)
        cin = hstate[0:1, cols]
        h_in = jnp.zeros((SUBLANES, bw), F32)
        for s in range(SUBLANES):
            h_in = jnp.where(sublane == s, cin, h_in)
            cin = hl[s:s + 1] + pp[s:s + 1] * cin
        hstate[:, cols] = jnp.broadcast_to(cin, (SUBLANES, bw))
        blocks.append(jnp.concatenate([hls[j] + pps[j] * h_in for j in range(seg)], axis=0))

    hs = jnp.concatenate(blocks, axis=1)
    y = (hs * _gelu(gate)).astype(BF16)
    _from_segment_rows(hseg[...] + _dot(y, wout_ref[...]), stage, o_ref)


def _recurrent_block(h3, g, win, cw, cb, wa, ba, wx, bx, ap, wout, tm=256):
    B, S, D = h3.shape
    C = wout.shape[0]
    taps = cw.shape[0]
    return pl.pallas_call(
        functools.partial(_rec_kernel, taps=taps),
        grid=(B, S // tm),
        in_specs=[
            pl.BlockSpec((None, tm, D), lambda b, i: (b, i, 0)),
            _resident((1, D)), _resident(win.shape), _resident(cw.shape), _resident((1, C)),
            _resident(wa.shape), _resident((1, C)), _resident(wx.shape), _resident((1, C)),
            _resident((1, C)), _resident(wout.shape),
        ],
        out_specs=pl.BlockSpec((None, tm, D), lambda b, i: (b, i, 0)),
        out_shape=jax.ShapeDtypeStruct((B, S, D), F32),
        scratch_shapes=[
            pltpu.VMEM(_stage_shape(tm, D), F32),
            pltpu.VMEM((tm, D), F32),
            pltpu.VMEM((taps - 1, SUBLANES, C), F32),
            pltpu.VMEM((SUBLANES, C), F32),
        ],
        compiler_params=_params(("arbitrary", "arbitrary")),
        name="recurrent_block",
    )(h3, g, win, cw, cb, wa, ba, wx, bx, ap, wout)


def _alibi_slopes(n):
    return jnp.exp2(-8.0 * jnp.arange(1, n + 1, dtype=F32) / n)


def _row(v):
    return v.reshape(1, -1).astype(F32)


def kernel(x, attn_norm, attn_w_in, attn_w_out, a_q_norm, a_k_norm, b_q_norm, b_k_norm, b_sub_norm,
           b_lam_q1, b_lam_k1, b_lam_q2, b_lam_k2, rec_norm, rec_w_in, rec_conv_w, rec_conv_b,
           rec_gate_a_w, rec_gate_a_b, rec_gate_x_w, rec_gate_x_b, rec_a_param, rec_w_out,
           ffn_norm, ffn_w_up, ffn_conv_w, ffn_conv_b, ffn_w_down):
    B, S, D = x.shape
    depth = ffn_norm.shape[0]
    slopes = _alibi_slopes(A_HEADS + B_HEADS)
    h = x
    for layer in range(depth):
        j = layer // 2
        if layer % 2 == 0:
            lam_init = 0.8 - 0.6 * math.exp(-0.3 * layer)
            scale = HEAD_DIM ** -0.5
            reps = SEC // HEAD_DIM
            ones = jnp.ones((SEC,), F32)
            head_gains = jnp.stack([
                jnp.tile(a_q_norm[j].astype(F32), reps) * (scale * LOG2E),
                jnp.tile(a_k_norm[j].astype(F32), reps), ones,
                jnp.tile(b_q_norm[j].astype(F32), reps) * (scale * LOG2E),
                jnp.tile(b_k_norm[j].astype(F32), reps), ones])
            proj = _attn_inproj(h.reshape(B * S, D), _row(attn_norm[j]),
                                attn_w_in[j].astype(BF16), head_gains)
            proj3 = proj.reshape(B, S, -1)
            oa = _dilated_attention(proj3, slopes[:A_HEADS] * LOG2E)
            lam_vecs = jnp.stack([b_lam_q1[j], b_lam_k1[j], b_lam_q2[j], b_lam_k2[j]]).astype(F32)
            ob = _diff_attention(proj3, slopes[A_HEADS:] * LOG2E, lam_vecs, _row(b_sub_norm[j]),
                                 lam_init)
            h = _attn_outproj(h.reshape(B * S, D), oa.reshape(B * S, -1), ob.reshape(B * S, -1),
                              attn_w_out[j].astype(BF16)).reshape(B, S, D)
        else:
            h = _recurrent_block(
                h, _row(rec_norm[j]), rec_w_in[j].astype(BF16), rec_conv_w[j].astype(F32),
                _row(rec_conv_b[j]), rec_gate_a_w[j].astype(BF16), _row(rec_gate_a_b[j]),
                rec_gate_x_w[j].astype(BF16), _row(rec_gate_x_b[j]), _row(rec_a_param[j]),
                rec_w_out[j].astype(BF16))
        h = _conv_ffn(h, _row(ffn_norm[layer]), ffn_w_up[layer].astype(BF16),
                      ffn_conv_w[layer].astype(F32), _row(ffn_conv_b[layer]),
                      ffn_w_down[layer].astype(BF16))
    return h
```

```python
import functools
import math

import numpy as np
import jax
import jax.numpy as jnp
from jax import lax
from jax.experimental import pallas as pl
from jax.experimental.pallas import tpu as pltpu

F32 = jnp.float32
BF16 = jnp.bfloat16

HEAD_DIM = 64
A_HEADS = 8
B_HEADS = 4
DILATED_CONFIGS = ((128, 1), (512, 4), (2048, 16))
BAND = 128
MAX_WINDOW = 2048
LRU_BLOCKS = 8
LRU_C = 8.0
NORM_EPS = 1e-6
NEG_INF = -1e30
LOG2E = math.log2(math.e)
LANES = 128
SUBLANES = 8
VMEM_LIMIT = 56 * 1024 * 1024


def _gelu(x):
    c = math.sqrt(2.0 / math.pi)
    return x * (0.5 * (1.0 + jnp.tanh(c * (x + 0.044715 * (x * x * x)))))


def _sigmoid(x):
    return 1.0 / (1.0 + jnp.exp(-x))


def _rms_rows(x, g):
    ms = jnp.mean(x * x, axis=-1, keepdims=True)
    return x * lax.rsqrt(ms + NORM_EPS) * g


def _dot(a, b):
    return jnp.dot(a, b, preferred_element_type=F32)


def _dot_nt(a, b):
    return lax.dot_general(a, b, (((1,), (1,)), ((), ())), preferred_element_type=F32)


def _params(sem):
    return pltpu.CompilerParams(dimension_semantics=sem, vmem_limit_bytes=VMEM_LIMIT)


SEC = 512
N_SEC = 6


def _inproj_kernel(x_ref, g_ref, w_ref, hg_ref, p_ref, o_ref):
    xn = _rms_rows(x_ref[...], g_ref[...]).astype(BF16)
    for s in range(N_SEC):
        y = _dot(xn, w_ref[:, s * SEC:(s + 1) * SEC])
        if s % 3 == 2:
            o_ref[:, s * SEC:(s + 1) * SEC] = y.astype(BF16)
        else:
            ms = _dot((y * y).astype(BF16), p_ref[...])
            o_ref[:, s * SEC:(s + 1) * SEC] = (
                y * lax.rsqrt(ms + NORM_EPS) * hg_ref[s:s + 1, :]).astype(BF16)


def _attn_inproj(x2, g, w, head_gains, tm=512):
    T, D = x2.shape
    N = w.shape[1]
    blk = np.kron(np.eye(SEC // HEAD_DIM), np.full((HEAD_DIM, HEAD_DIM), 1.0 / HEAD_DIM))
    pmat = jnp.asarray(blk, dtype=BF16)
    return pl.pallas_call(
        _inproj_kernel,
        grid=(T // tm,),
        in_specs=[
            pl.BlockSpec((tm, D), lambda i: (i, 0)),
            pl.BlockSpec((1, D), lambda i: (0, 0)),
            pl.BlockSpec((D, N), lambda i: (0, 0)),
            pl.BlockSpec((N_SEC, SEC), lambda i: (0, 0)),
            pl.BlockSpec((SEC, SEC), lambda i: (0, 0)),
        ],
        out_specs=pl.BlockSpec((tm, N), lambda i: (i, 0)),
        out_shape=jax.ShapeDtypeStruct((T, N), BF16),
        compiler_params=_params(("parallel",)),
        name="attn_inproj",
    )(x2, g, w, head_gains, pmat)


def _dilated_kernel(slopes_ref, q_ref, k_ref, v_ref, o_ref, qf, kf, vf, ob, lb, bias_buf):
    S = q_ref.shape[0]
    pad = kf.shape[0] - S
    g = pl.program_id(1)
    qf[...] = q_ref[...].astype(F32)
    kf[0:pad, :] = jnp.zeros((pad, LANES), F32)
    vf[0:pad, :] = jnp.zeros((pad, LANES), F32)
    kf[pad:, :] = k_ref[...].astype(F32)
    vf[pad:, :] = v_ref[...].astype(F32)

    lo = lax.broadcasted_iota(jnp.int32, (BAND, LANES), 1) < HEAD_DIM
    ii = lax.broadcasted_iota(jnp.int32, (BAND, 2 * BAND), 0)
    jj = lax.broadcasted_iota(jnp.int32, (BAND, 2 * BAND), 1)
    delta = ii + BAND - jj
    in_band = (delta >= 0) & (delta <= BAND)
    prev_half = lax.broadcasted_iota(jnp.int32, (2 * BAND, 2 * BAND), 1) < BAND
    sl0 = slopes_ref[2 * g]
    sl1 = slopes_ref[2 * g + 1]
    ones = jnp.ones((2 * BAND, LANES), BF16)

    for bi, (window, d) in enumerate(DILATED_CONFIGS):
        span = BAND * d
        nbs = S // span
        dist = (delta * d).astype(F32)
        bias = jnp.concatenate(
            [jnp.where(in_band, -sl0 * dist, NEG_INF),
             jnp.where(in_band, -sl1 * dist, NEG_INF)], axis=0)
        bias_buf[2 * bi] = bias
        bias_buf[2 * bi + 1] = jnp.where(prev_half, NEG_INF, bias)

        def body(blk, carry, d=d, span=span, nbs=nbs, bi=bi):
            r = blk // nbs
            nb = blk - r * nbs
            start = nb * span + r
            q = qf[pl.ds(start, BAND, stride=d), :].astype(BF16)
            k = kf[pl.ds(pad + start - span, 2 * BAND, stride=d), :].astype(BF16)
            v = vf[pl.ds(pad + start - span, 2 * BAND, stride=d), :].astype(BF16)
            zero = jnp.zeros_like(q)
            q2 = jnp.concatenate([jnp.where(lo, q, zero), jnp.where(lo, zero, q)], axis=0)
            s = _dot_nt(q2, k) + bias_buf[2 * bi + jnp.where(nb == 0, 1, 0)]
            m = jnp.max(s, axis=1, keepdims=True)
            e = jnp.exp2(s - m)
            pv = _dot(e.astype(BF16), jnp.concatenate([v, ones], axis=1))
            den = pv[:, LANES:]
            o2 = pv[:, :LANES] / den
            lse = m + jnp.log2(den)
            ob[bi, pl.ds(start, BAND, stride=d), :] = jnp.where(lo, o2[:BAND], o2[BAND:])
            lb[bi, pl.ds(start, BAND, stride=d), :] = jnp.where(lo, lse[:BAND], lse[BAND:])
            return carry

        lax.fori_loop(0, S // BAND, body, 0, unroll=8)

    rows = 512
    def mix(c, carry):
        sl = pl.ds(pl.multiple_of(c * rows, rows), rows)
        l0, l1, l2 = lb[0, sl, :], lb[1, sl, :], lb[2, sl, :]
        m = jnp.maximum(jnp.maximum(l0, l1), l2)
        w0, w1, w2 = jnp.exp2(l0 - m), jnp.exp2(l1 - m), jnp.exp2(l2 - m)
        tot = w0 + w1 + w2
        o = (w0 * ob[0, sl, :] + w1 * ob[1, sl, :] + w2 * ob[2, sl, :]) / tot
        o_ref[sl, :] = o.astype(o_ref.dtype)
        return carry
    lax.fori_loop(0, S // rows, mix, 0)


def _dilated_attention(proj3, slopes):
    B, S, _ = proj3.shape
    assert S % MAX_WINDOW == 0
    pairs = A_HEADS // 2
    blk = lambda off: pl.BlockSpec((None, S, LANES), lambda b, g, off=off: (b, 0, off + g))
    return pl.pallas_call(
        _dilated_kernel,
        grid=(B, pairs),
        in_specs=[
            pl.BlockSpec(memory_space=pltpu.SMEM),
            blk(0), blk(pairs), blk(2 * pairs),
        ],
        out_specs=pl.BlockSpec((None, S, LANES), lambda b, g: (b, 0, g)),
        out_shape=jax.ShapeDtypeStruct((B, S, A_HEADS * HEAD_DIM), BF16),
        scratch_shapes=[
            pltpu.VMEM((S, LANES), F32),
            pltpu.VMEM((MAX_WINDOW + S, LANES), F32),
            pltpu.VMEM((MAX_WINDOW + S, LANES), F32),
            pltpu.VMEM((3, S, LANES), F32),
            pltpu.VMEM((3, S, LANES), F32),
            pltpu.VMEM((2 * len(DILATED_CONFIGS), 2 * BAND, 2 * BAND), F32),
        ],
        compiler_params=_params(("parallel", "parallel")),
        name="dilated_attn",
    )(slopes, proj3, proj3, proj3)


TQ = 512
KV_UNROLL = 2


def _diff_kernel(slopes_ref, q_ref, k_ref, v_ref, lam_ref, sg_ref, o_ref,
                 m_ref, acc_ref, *, lam_init):
    h = pl.program_id(1)
    qi = pl.program_id(2)
    slope = slopes_ref[h]
    q = q_ref[...]
    lo = lax.broadcasted_iota(jnp.int32, (TQ, LANES), 1) < HEAD_DIM
    zero = jnp.zeros_like(q)
    q2 = jnp.concatenate([jnp.where(lo, q, zero), jnp.where(lo, zero, q)], axis=0)
    kcol = slope * lax.broadcasted_iota(jnp.int32, (1, TQ), 1).astype(F32)
    ones = jnp.ones((TQ, LANES), BF16)

    m_ref[...] = jnp.full(m_ref.shape, NEG_INF, F32)
    acc_ref[...] = jnp.zeros(acc_ref.shape, F32)

    def step(j, masked):
        ks = pl.ds(pl.multiple_of(j * TQ, TQ), TQ)
        k = k_ref[ks, :]
        v1 = jnp.concatenate([v_ref[ks, :], ones], axis=1)
        s = _dot_nt(q2, k) + (kcol + slope * (j * TQ).astype(F32))
        if masked:
            rel = (lax.broadcasted_iota(jnp.int32, (2 * TQ, TQ), 0) % TQ
                   - lax.broadcasted_iota(jnp.int32, (2 * TQ, TQ), 1))
            s = jnp.where(rel >= 0, s, NEG_INF)
        m_old = m_ref[...]
        m_new = jnp.maximum(m_old, jnp.max(s, axis=1, keepdims=True))
        alpha = jnp.exp2(m_old - m_new)
        e = jnp.exp2(s - jnp.concatenate([m_new] * (TQ // LANES), axis=1))
        acc_ref[...] = (jnp.concatenate([alpha, alpha], axis=1) * acc_ref[...]
                        + _dot(e.astype(BF16), v1))
        m_ref[...] = m_new

    def body(jq, carry):
        for u in range(KV_UNROLL):
            step(KV_UNROLL * jq + u, False)
        return carry
    nq = qi // KV_UNROLL
    lax.fori_loop(0, nq, body, 0)

    rem = qi - nq * KV_UNROLL
    for r in range(KV_UNROLL):
        @pl.when(rem == r)
        def _(r=r):
            for u in range(r):
                step(qi - r + u, False)
            step(qi, True)

    lq = lam_ref[...]
    lam = (jnp.exp(jnp.sum(lq[0:1] * lq[1:2], axis=1, keepdims=True))
           - jnp.exp(jnp.sum(lq[2:3] * lq[3:4], axis=1, keepdims=True)) + lam_init)
    on = acc_ref[:, 0:LANES] / acc_ref[:, LANES:]
    o = on[:TQ] - lam * on[TQ:]
    o = _rms_rows(o, sg_ref[...]) * (1.0 - lam_init)
    o_ref[...] = o.astype(o_ref.dtype)


def _diff_attention(proj3, slopes, lam_vecs, sub_gain, lam_init):
    B, S, _ = proj3.shape
    qoff = 3 * A_HEADS * HEAD_DIM // LANES
    koff = qoff + B_HEADS
    voff = koff + B_HEADS
    return pl.pallas_call(
        functools.partial(_diff_kernel, lam_init=lam_init),
        grid=(B, B_HEADS, S // TQ),
        in_specs=[
            pl.BlockSpec(memory_space=pltpu.SMEM),
            pl.BlockSpec((None, TQ, LANES), lambda b, h, i: (b, i, qoff + h)),
            pl.BlockSpec((None, S, LANES), lambda b, h, i: (b, 0, koff + h)),
            pl.BlockSpec((None, S, LANES), lambda b, h, i: (b, 0, voff + h)),
            pl.BlockSpec((4, HEAD_DIM), lambda b, h, i: (0, 0)),
            pl.BlockSpec((1, 2 * HEAD_DIM), lambda b, h, i: (0, 0)),
        ],
        out_specs=pl.BlockSpec((None, TQ, LANES), lambda b, h, i: (b, i, h)),
        out_shape=jax.ShapeDtypeStruct((B, S, B_HEADS * 2 * HEAD_DIM), BF16),
        scratch_shapes=[
            pltpu.VMEM((2 * TQ, LANES), F32),
            pltpu.VMEM((2 * TQ, 2 * LANES), F32),
        ],
        compiler_params=_params(("parallel", "parallel", "parallel")),
        name="diff_attn",
    )(slopes, proj3, proj3, proj3, lam_vecs, sub_gain)


FFN_CHUNK = 512


def _seg_pitch(seg):
    p = seg // SUBLANES + 1
    return SUBLANES * (p if p % 2 else p + 1)


def _to_segment_rows(src, stage, dst_ref):
    tm, C = src.shape
    seg = tm // SUBLANES
    pitch = _seg_pitch(seg)
    for n in range(C // LANES):
        cols = slice(n * LANES, (n + 1) * LANES)
        for s in range(SUBLANES):
            stage[n, s * pitch:s * pitch + seg, :] = src[s * seg:(s + 1) * seg, cols]
    for j in range(seg):
        dst_ref[j * SUBLANES:(j + 1) * SUBLANES, :] = jnp.concatenate(
            [stage[n, pl.ds(j, SUBLANES, stride=pitch), :] for n in range(C // LANES)], axis=1)


def _from_segment_rows(val, stage, dst_ref):
    tm, C = val.shape
    seg = tm // SUBLANES
    pitch = _seg_pitch(seg)
    for n in range(C // LANES):
        cols = slice(n * LANES, (n + 1) * LANES)
        for j in range(seg):
            stage[n, pl.ds(j, SUBLANES, stride=pitch), :] = val[j * SUBLANES:(j + 1) * SUBLANES, cols]
        for s in range(SUBLANES):
            dst_ref[s * seg:(s + 1) * seg, cols] = stage[n, s * pitch:s * pitch + seg, :]


def _segment_conv(u, carry_ref, cols, w_ref, b_ref, taps):
    tm = u.shape[0]
    first_sublane = lax.broadcasted_iota(jnp.int32, (SUBLANES, u.shape[1]), 0) == 0
    wrapped = []
    for i in range(1, taps):
        tail = u[tm - i * SUBLANES:tm - (i - 1) * SUBLANES]
        wrapped.append(jnp.where(first_sublane,
                                 pltpu.roll(carry_ref[i - 1, :, cols], 1, 0),
                                 pltpu.roll(tail, 1, 0)))
        carry_ref[i - 1, :, cols] = tail
    out = b_ref[:, cols] + w_ref[taps - 1:taps, cols] * u
    for k in range(1, taps):
        shifted = jnp.concatenate(wrapped[k - 1::-1] + [u[:tm - k * SUBLANES]], axis=0)
        out = out + w_ref[taps - 1 - k:taps - k, cols] * shifted
    return out


def _ffn_kernel(*refs, taps, with_attn):
    if with_attn:
        x_ref, a_ref, b_ref, wo_ref, *refs = refs
        na = a_ref.shape[1]
        h = x_ref[...] + _dot(a_ref[...], wo_ref[0:na, :]) + _dot(b_ref[...], wo_ref[na:, :])
    else:
        h, *refs = refs
    g_ref, wup_ref, cw_ref, cb_ref, wdn_ref, o_ref, stage, hseg, carry, act = refs
    F = wdn_ref.shape[0]

    @pl.when(pl.program_id(1) == 0)
    def _():
        carry[...] = jnp.zeros(carry.shape, F32)

    _to_segment_rows(h, stage, hseg)
    xn = _rms_rows(hseg[...], g_ref[...]).astype(BF16)
    for c in range(F // FFN_CHUNK):
        gc = slice(c * FFN_CHUNK, (c + 1) * FFN_CHUNK)
        vc = slice(F + c * FFN_CHUNK, F + (c + 1) * FFN_CHUNK)
        gg = _segment_conv(_dot(xn, wup_ref[:, gc]), carry, gc, cw_ref, cb_ref, taps)
        vv = _segment_conv(_dot(xn, wup_ref[:, vc]), carry, vc, cw_ref, cb_ref, taps)
        act[:, gc] = (_gelu(gg) * vv).astype(BF16)
    _from_segment_rows(hseg[...] + _dot(act[...], wdn_ref[...]), stage, o_ref)


def _stage_shape(tm, C):
    return (C // LANES, SUBLANES * _seg_pitch(tm // SUBLANES), LANES)


def _resident(shape):
    return pl.BlockSpec(shape, lambda b, i: (0,) * len(shape), pipeline_mode=pl.Buffered(1))


def _conv_ffn(h3, g, wup, cw, cb, wdn, attn=None, tm=512):
    B, S, D = h3.shape
    F2 = wup.shape[1]
    taps = cw.shape[0]
    row_tile = lambda width: pl.BlockSpec((None, tm, width), lambda b, i: (b, i, 0))
    attn_args, attn_specs = (), []
    if attn is not None:
        oa, ob, wo = attn
        attn_args = (oa, ob, wo)
        attn_specs = [row_tile(oa.shape[2]), row_tile(ob.shape[2]), _resident(wo.shape)]
    return pl.pallas_call(
        functools.partial(_ffn_kernel, taps=taps, with_attn=attn is not None),
        grid=(B, S // tm),
        in_specs=[
            row_tile(D),
            *attn_specs,
            _resident((1, D)),
            _resident(wup.shape),
            _resident(cw.shape),
            _resident((1, F2)),
            _resident(wdn.shape),
        ],
        out_specs=row_tile(D),
        out_shape=jax.ShapeDtypeStruct((B, S, D), F32),
        scratch_shapes=[
            pltpu.VMEM(_stage_shape(tm, D), F32),
            pltpu.VMEM((tm, D), F32),
            pltpu.VMEM((taps - 1, SUBLANES, F2), F32),
            pltpu.VMEM((tm, F2 // 2), BF16),
        ],
        compiler_params=_params(("arbitrary", "arbitrary")),
        name="conv_ffn",
    )(h3, *attn_args, g, wup, cw, cb, wdn)


def _rec_kernel(h_ref, g_ref, win_ref, cw_ref, cb_ref, wa_ref, ba_ref, wx_ref, bx_ref,
                ap_ref, wout_ref, o_ref, stage, hseg, carry, hstate, *, taps):
    tm, C = h_ref.shape
    seg = tm // SUBLANES
    bw = C // LRU_BLOCKS

    @pl.when(pl.program_id(1) == 0)
    def _():
        carry[...] = jnp.zeros(carry.shape, F32)
        hstate[...] = jnp.zeros(hstate.shape, F32)

    _to_segment_rows(h_ref, stage, hseg)
    xn = _rms_rows(hseg[...], g_ref[...]).astype(BF16)
    gate = _dot(xn, win_ref[:, 0:C])
    xr_all = _segment_conv(_dot(xn, win_ref[:, C:]), carry, slice(0, C), cw_ref, cb_ref, taps)

    ap = ap_ref[...]
    decay = -LRU_C * (jnp.maximum(-ap, 0.0) + jnp.log1p(jnp.exp(-jnp.abs(ap))))
    sublane = lax.broadcasted_iota(jnp.int32, (SUBLANES, bw), 0)
    blocks = []
    for n in range(LRU_BLOCKS):
        cols = slice(n * bw, (n + 1) * bw)
        xr = xr_all[:, cols]
        xb = xr.astype(BF16)
        r = _sigmoid(_dot(xb, wa_ref[n]) + ba_ref[:, cols])
        i = _sigmoid(_dot(xb, wx_ref[n]) + bx_ref[:, cols])
        log_a = decay[:, cols] * r
        a = jnp.exp(log_a)
        u = jnp.sqrt(-jnp.tanh(log_a) * (1.0 + a * a)) * (i * xr)

        hl = jnp.zeros((SUBLANES, bw), F32)
        pp = jnp.ones((SUBLANES, bw), F32)
        hls, pps = [], []
        for j in range(seg):
            aj = a[j * SUBLANES:(j + 1) * SUBLANES]
            hl = aj * hl + u[j * SUBLANES:(j + 1) * SUBLANES]
            pp = aj * pp
            hls.append(hl)
            pps.append(pp)
        cin = hstate[0:1, cols]
        h_in = jnp.zeros((SUBLANES, bw), F32)
        for s in range(SUBLANES):
            h_in = jnp.where(sublane == s, cin, h_in)
            cin = hl[s:s + 1] + pp[s:s + 1] * cin
        hstate[:, cols] = jnp.broadcast_to(cin, (SUBLANES, bw))
        blocks.append(jnp.concatenate([hls[j] + pps[j] * h_in for j in range(seg)], axis=0))

    hs = jnp.concatenate(blocks, axis=1)
    y = (hs * _gelu(gate)).astype(BF16)
    _from_segment_rows(hseg[...] + _dot(y, wout_ref[...]), stage, o_ref)


def _recurrent_block(h3, g, win, cw, cb, wa, ba, wx, bx, ap, wout, tm=512):
    B, S, D = h3.shape
    C = wout.shape[0]
    taps = cw.shape[0]
    return pl.pallas_call(
        functools.partial(_rec_kernel, taps=taps),
        grid=(B, S // tm),
        in_specs=[
            pl.BlockSpec((None, tm, D), lambda b, i: (b, i, 0)),
            _resident((1, D)), _resident(win.shape), _resident(cw.shape), _resident((1, C)),
            _resident(wa.shape), _resident((1, C)), _resident(wx.shape), _resident((1, C)),
            _resident((1, C)), _resident(wout.shape),
        ],
        out_specs=pl.BlockSpec((None, tm, D), lambda b, i: (b, i, 0)),
        out_shape=jax.ShapeDtypeStruct((B, S, D), F32),
        scratch_shapes=[
            pltpu.VMEM(_stage_shape(tm, D), F32),
            pltpu.VMEM((tm, D), F32),
            pltpu.VMEM((taps - 1, SUBLANES, C), F32),
            pltpu.VMEM((SUBLANES, C), F32),
        ],
        compiler_params=_params(("arbitrary", "arbitrary")),
        name="recurrent_block",
    )(h3, g, win, cw, cb, wa, ba, wx, bx, ap, wout)


def _alibi_slopes(n):
    return jnp.exp2(-8.0 * jnp.arange(1, n + 1, dtype=F32) / n)


def _row(v):
    return v.reshape(1, -1).astype(F32)


def kernel(x, attn_norm, attn_w_in, attn_w_out, a_q_norm, a_k_norm, b_q_norm, b_k_norm, b_sub_norm,
           b_lam_q1, b_lam_k1, b_lam_q2, b_lam_k2, rec_norm, rec_w_in, rec_conv_w, rec_conv_b,
           rec_gate_a_w, rec_gate_a_b, rec_gate_x_w, rec_gate_x_b, rec_a_param, rec_w_out,
           ffn_norm, ffn_w_up, ffn_conv_w, ffn_conv_b, ffn_w_down):
    B, S, D = x.shape
    depth = ffn_norm.shape[0]
    slopes = _alibi_slopes(A_HEADS + B_HEADS)
    h = x
    for layer in range(depth):
        j = layer // 2
        attn = None
        if layer % 2 == 0:
            lam_init = 0.8 - 0.6 * math.exp(-0.3 * layer)
            scale = HEAD_DIM ** -0.5
            reps = SEC // HEAD_DIM
            ones = jnp.ones((SEC,), F32)
            head_gains = jnp.stack([
                jnp.tile(a_q_norm[j].astype(F32), reps) * (scale * LOG2E),
                jnp.tile(a_k_norm[j].astype(F32), reps), ones,
                jnp.tile(b_q_norm[j].astype(F32), reps) * (scale * LOG2E),
                jnp.tile(b_k_norm[j].astype(F32), reps), ones])
            proj = _attn_inproj(h.reshape(B * S, D), _row(attn_norm[j]),
                                attn_w_in[j].astype(BF16), head_gains)
            proj3 = proj.reshape(B, S, -1)
            oa = _dilated_attention(proj3, slopes[:A_HEADS] * LOG2E)
            lam_vecs = jnp.stack([b_lam_q1[j], b_lam_k1[j], b_lam_q2[j], b_lam_k2[j]]).astype(F32)
            ob = _diff_attention(proj3, slopes[A_HEADS:] * LOG2E, lam_vecs, _row(b_sub_norm[j]),
                                 lam_init)
            attn = (oa, ob, attn_w_out[j].astype(BF16))
        else:
            h = _recurrent_block(
                h, _row(rec_norm[j]), rec_w_in[j].astype(BF16), rec_conv_w[j].astype(F32),
                _row(rec_conv_b[j]), rec_gate_a_w[j].astype(BF16), _row(rec_gate_a_b[j]),
                rec_gate_x_w[j].astype(BF16), _row(rec_gate_x_b[j]), _row(rec_a_param[j]),
                rec_w_out[j].astype(BF16))
        h = _conv_ffn(h, _row(ffn_norm[layer]), ffn_w_up[layer].astype(BF16),
                      ffn_conv_w[layer].astype(F32), _row(ffn_conv_b[layer]),
                      ffn_w_down[layer].astype(BF16), attn=attn)
    return h
```

```python
import functools
import math

import numpy as np
import jax
import jax.numpy as jnp
from jax import lax
from jax.experimental import pallas as pl
from jax.experimental.pallas import tpu as pltpu

F32 = jnp.float32
BF16 = jnp.bfloat16

HEAD_DIM = 64
A_HEADS = 8
B_HEADS = 4
DILATED_CONFIGS = ((128, 1), (512, 4), (2048, 16))
BAND = 128
MAX_WINDOW = 2048
LRU_BLOCKS = 8
LRU_C = 8.0
NORM_EPS = 1e-6
NEG_INF = -1e30
LOG2E = math.log2(math.e)
LANES = 128
SUBLANES = 8
VMEM_LIMIT = 56 * 1024 * 1024


def _gelu(x):
    c = math.sqrt(2.0 / math.pi)
    return x * (0.5 * (1.0 + jnp.tanh(c * (x + 0.044715 * (x * x * x)))))


def _sigmoid(x):
    return 1.0 / (1.0 + jnp.exp(-x))


def _rms_rows(x, g):
    ms = jnp.mean(x * x, axis=-1, keepdims=True)
    return x * lax.rsqrt(ms + NORM_EPS) * g


def _dot(a, b):
    return jnp.dot(a, b, preferred_element_type=F32)


def _dot_nt(a, b):
    return lax.dot_general(a, b, (((1,), (1,)), ((), ())), preferred_element_type=F32)


def _params(sem):
    return pltpu.CompilerParams(dimension_semantics=sem, vmem_limit_bytes=VMEM_LIMIT)


SEC = 512
N_SEC = 6


def _inproj_kernel(x_ref, g_ref, w_ref, hg_ref, p_ref, o_ref):
    xn = _rms_rows(x_ref[...], g_ref[...]).astype(BF16)
    for s in range(N_SEC):
        y = _dot(xn, w_ref[:, s * SEC:(s + 1) * SEC])
        if s % 3 == 2:
            o_ref[:, s * SEC:(s + 1) * SEC] = y.astype(BF16)
        else:
            ms = _dot((y * y).astype(BF16), p_ref[...])
            o_ref[:, s * SEC:(s + 1) * SEC] = (
                y * lax.rsqrt(ms + NORM_EPS) * hg_ref[s:s + 1, :]).astype(BF16)


def _attn_inproj(x2, g, w, head_gains, tm=512):
    T, D = x2.shape
    N = w.shape[1]
    blk = np.kron(np.eye(SEC // HEAD_DIM), np.full((HEAD_DIM, HEAD_DIM), 1.0 / HEAD_DIM))
    pmat = jnp.asarray(blk, dtype=BF16)
    return pl.pallas_call(
        _inproj_kernel,
        grid=(T // tm,),
        in_specs=[
            pl.BlockSpec((tm, D), lambda i: (i, 0)),
            pl.BlockSpec((1, D), lambda i: (0, 0)),
            pl.BlockSpec((D, N), lambda i: (0, 0)),
            pl.BlockSpec((N_SEC, SEC), lambda i: (0, 0)),
            pl.BlockSpec((SEC, SEC), lambda i: (0, 0)),
        ],
        out_specs=pl.BlockSpec((tm, N), lambda i: (i, 0)),
        out_shape=jax.ShapeDtypeStruct((T, N), BF16),
        compiler_params=_params(("parallel",)),
        name="attn_inproj",
    )(x2, g, w, head_gains, pmat)


def _dilated_kernel(slopes_ref, q_ref, k_ref, v_ref, o_ref, qf, kf, vf, ob, lb, bias_buf):
    S = q_ref.shape[0]
    pad = kf.shape[0] - S
    g = pl.program_id(1)
    qf[...] = q_ref[...].astype(F32)
    kf[0:pad, :] = jnp.zeros((pad, LANES), F32)
    vf[0:pad, :] = jnp.zeros((pad, LANES), F32)
    kf[pad:, :] = k_ref[...].astype(F32)
    vf[pad:, :] = v_ref[...].astype(F32)

    lo = lax.broadcasted_iota(jnp.int32, (BAND, LANES), 1) < HEAD_DIM
    ii = lax.broadcasted_iota(jnp.int32, (BAND, 2 * BAND), 0)
    jj = lax.broadcasted_iota(jnp.int32, (BAND, 2 * BAND), 1)
    delta = ii + BAND - jj
    in_band = (delta >= 0) & (delta <= BAND)
    prev_half = lax.broadcasted_iota(jnp.int32, (2 * BAND, 2 * BAND), 1) < BAND
    sl0 = slopes_ref[2 * g]
    sl1 = slopes_ref[2 * g + 1]
    ones = jnp.ones((2 * BAND, LANES), BF16)

    for bi, (window, d) in enumerate(DILATED_CONFIGS):
        span = BAND * d
        nbs = S // span
        dist = (delta * d).astype(F32)
        bias = jnp.concatenate(
            [jnp.where(in_band, -sl0 * dist, NEG_INF),
             jnp.where(in_band, -sl1 * dist, NEG_INF)], axis=0)
        bias_buf[2 * bi] = bias
        bias_buf[2 * bi + 1] = jnp.where(prev_half, NEG_INF, bias)

        def body(blk, carry, d=d, span=span, nbs=nbs, bi=bi):
            r = blk // nbs
            nb = blk - r * nbs
            start = nb * span + r
            q = qf[pl.ds(start, BAND, stride=d), :].astype(BF16)
            k = kf[pl.ds(pad + start - span, 2 * BAND, stride=d), :].astype(BF16)
            v = vf[pl.ds(pad + start - span, 2 * BAND, stride=d), :].astype(BF16)
            zero = jnp.zeros_like(q)
            q2 = jnp.concatenate([jnp.where(lo, q, zero), jnp.where(lo, zero, q)], axis=0)
            s = _dot_nt(q2, k) + bias_buf[2 * bi + jnp.where(nb == 0, 1, 0)]
            m = jnp.max(s, axis=1, keepdims=True)
            e = jnp.exp2(s - m)
            pv = _dot(e.astype(BF16), jnp.concatenate([v, ones], axis=1))
            den = pv[:, LANES:]
            o2 = pv[:, :LANES] / den
            lse = m + jnp.log2(den)
            ob[bi, pl.ds(start, BAND, stride=d), :] = jnp.where(lo, o2[:BAND], o2[BAND:])
            lb[bi, pl.ds(start, BAND, stride=d), :] = jnp.where(lo, lse[:BAND], lse[BAND:])
            return carry

        lax.fori_loop(0, S // BAND, body, 0, unroll=8)

    rows = 512
    def mix(c, carry):
        sl = pl.ds(pl.multiple_of(c * rows, rows), rows)
        l0, l1, l2 = lb[0, sl, :], lb[1, sl, :], lb[2, sl, :]
        m = jnp.maximum(jnp.maximum(l0, l1), l2)
        w0, w1, w2 = jnp.exp2(l0 - m), jnp.exp2(l1 - m), jnp.exp2(l2 - m)
        tot = w0 + w1 + w2
        o = (w0 * ob[0, sl, :] + w1 * ob[1, sl, :] + w2 * ob[2, sl, :]) / tot
        o_ref[sl, :] = o.astype(o_ref.dtype)
        return carry
    lax.fori_loop(0, S // rows, mix, 0)


def _dilated_attention(proj3, slopes):
    B, S, _ = proj3.shape
    assert S % MAX_WINDOW == 0
    pairs = A_HEADS // 2
    blk = lambda off: pl.BlockSpec((None, S, LANES), lambda b, g, off=off: (b, 0, off + g))
    return pl.pallas_call(
        _dilated_kernel,
        grid=(B, pairs),
        in_specs=[
            pl.BlockSpec(memory_space=pltpu.SMEM),
            blk(0), blk(pairs), blk(2 * pairs),
        ],
        out_specs=pl.BlockSpec((None, S, LANES), lambda b, g: (b, 0, g)),
        out_shape=jax.ShapeDtypeStruct((B, S, A_HEADS * HEAD_DIM), BF16),
        scratch_shapes=[
            pltpu.VMEM((S, LANES), F32),
            pltpu.VMEM((MAX_WINDOW + S, LANES), F32),
            pltpu.VMEM((MAX_WINDOW + S, LANES), F32),
            pltpu.VMEM((3, S, LANES), F32),
            pltpu.VMEM((3, S, LANES), F32),
            pltpu.VMEM((2 * len(DILATED_CONFIGS), 2 * BAND, 2 * BAND), F32),
        ],
        compiler_params=_params(("parallel", "parallel")),
        name="dilated_attn",
    )(slopes, proj3, proj3, proj3)


TQ = 512
KV_UNROLL = 4


def _diff_kernel(slopes_ref, q_ref, k_ref, v_ref, lam_ref, sg_ref, o_ref,
                 m_ref, acc_ref, *, lam_init):
    h = pl.program_id(1)
    qi = pl.program_id(2)
    slope = slopes_ref[h]
    q = q_ref[...]
    lo = lax.broadcasted_iota(jnp.int32, (TQ, LANES), 1) < HEAD_DIM
    zero = jnp.zeros_like(q)
    q2 = jnp.concatenate([jnp.where(lo, q, zero), jnp.where(lo, zero, q)], axis=0)
    kcol = slope * lax.broadcasted_iota(jnp.int32, (1, TQ), 1).astype(F32)
    ones = jnp.ones((TQ, LANES), BF16)

    m_ref[...] = jnp.full(m_ref.shape, NEG_INF, F32)
    acc_ref[...] = jnp.zeros(acc_ref.shape, F32)

    def step(j, masked):
        ks = pl.ds(pl.multiple_of(j * TQ, TQ), TQ)
        k = k_ref[ks, :]
        v1 = jnp.concatenate([v_ref[ks, :], ones], axis=1)
        s = _dot_nt(q2, k) + (kcol + slope * (j * TQ).astype(F32))
        if masked:
            rel = (lax.broadcasted_iota(jnp.int32, (2 * TQ, TQ), 0) % TQ
                   - lax.broadcasted_iota(jnp.int32, (2 * TQ, TQ), 1))
            s = jnp.where(rel >= 0, s, NEG_INF)
        m_old = m_ref[...]
        m_new = jnp.maximum(m_old, jnp.max(s, axis=1, keepdims=True))
        alpha = jnp.exp2(m_old - m_new)
        e = jnp.exp2(s - jnp.concatenate([m_new] * (TQ // LANES), axis=1))
        acc_ref[...] = (jnp.concatenate([alpha, alpha], axis=1) * acc_ref[...]
                        + _dot(e.astype(BF16), v1))
        m_ref[...] = m_new

    def body(jq, carry):
        for u in range(KV_UNROLL):
            step(KV_UNROLL * jq + u, False)
        return carry
    nq = qi // KV_UNROLL
    lax.fori_loop(0, nq, body, 0)

    rem = qi - nq * KV_UNROLL
    for r in range(KV_UNROLL):
        @pl.when(rem == r)
        def _(r=r):
            for u in range(r):
                step(qi - r + u, False)
            step(qi, True)

    lq = lam_ref[...]
    lam = (jnp.exp(jnp.sum(lq[0:1] * lq[1:2], axis=1, keepdims=True))
           - jnp.exp(jnp.sum(lq[2:3] * lq[3:4], axis=1, keepdims=True)) + lam_init)
    on = acc_ref[:, 0:LANES] / acc_ref[:, LANES:]
    o = on[:TQ] - lam * on[TQ:]
    o = _rms_rows(o, sg_ref[...]) * (1.0 - lam_init)
    o_ref[...] = o.astype(o_ref.dtype)


def _diff_attention(proj3, slopes, lam_vecs, sub_gain, lam_init):
    B, S, _ = proj3.shape
    qoff = 3 * A_HEADS * HEAD_DIM // LANES
    koff = qoff + B_HEADS
    voff = koff + B_HEADS
    return pl.pallas_call(
        functools.partial(_diff_kernel, lam_init=lam_init),
        grid=(B, B_HEADS, S // TQ),
        in_specs=[
            pl.BlockSpec(memory_space=pltpu.SMEM),
            pl.BlockSpec((None, TQ, LANES), lambda b, h, i: (b, i, qoff + h)),
            pl.BlockSpec((None, S, LANES), lambda b, h, i: (b, 0, koff + h)),
            pl.BlockSpec((None, S, LANES), lambda b, h, i: (b, 0, voff + h)),
            pl.BlockSpec((4, HEAD_DIM), lambda b, h, i: (0, 0)),
            pl.BlockSpec((1, 2 * HEAD_DIM), lambda b, h, i: (0, 0)),
        ],
        out_specs=pl.BlockSpec((None, TQ, LANES), lambda b, h, i: (b, i, h)),
        out_shape=jax.ShapeDtypeStruct((B, S, B_HEADS * 2 * HEAD_DIM), BF16),
        scratch_shapes=[
            pltpu.VMEM((2 * TQ, LANES), F32),
            pltpu.VMEM((2 * TQ, 2 * LANES), F32),
        ],
        compiler_params=_params(("parallel", "parallel", "parallel")),
        name="diff_attn",
    )(slopes, proj3, proj3, proj3, lam_vecs, sub_gain)


FFN_CHUNK = 512


def _seg_pitch(seg):
    p = seg // SUBLANES + 1
    return SUBLANES * (p if p % 2 else p + 1)


def _to_segment_rows(src, stage, dst_ref):
    tm, C = src.shape
    seg = tm // SUBLANES
    pitch = _seg_pitch(seg)
    for n in range(C // LANES):
        cols = slice(n * LANES, (n + 1) * LANES)
        for s in range(SUBLANES):
            stage[n, s * pitch:s * pitch + seg, :] = src[s * seg:(s + 1) * seg, cols]
    for j in range(seg):
        dst_ref[j * SUBLANES:(j + 1) * SUBLANES, :] = jnp.concatenate(
            [stage[n, pl.ds(j, SUBLANES, stride=pitch), :] for n in range(C // LANES)], axis=1)


def _from_segment_rows(val, stage, dst_ref):
    tm, C = val.shape
    seg = tm // SUBLANES
    pitch = _seg_pitch(seg)
    for n in range(C // LANES):
        cols = slice(n * LANES, (n + 1) * LANES)
        for j in range(seg):
            stage[n, pl.ds(j, SUBLANES, stride=pitch), :] = val[j * SUBLANES:(j + 1) * SUBLANES, cols]
        for s in range(SUBLANES):
            dst_ref[s * seg:(s + 1) * seg, cols] = stage[n, s * pitch:s * pitch + seg, :]


def _segment_rows_ref(src, seg_in, stage, hseg):
    if seg_in:
        return src
    _to_segment_rows(src, stage, hseg)
    return hseg


def _store_segment_rows(val, seg_out, stage, o_ref):
    if seg_out:
        o_ref[...] = val
    else:
        _from_segment_rows(val, stage, o_ref)


def _segment_conv(u, carry_ref, cols, w_ref, b_ref, taps):
    tm = u.shape[0]
    first_sublane = lax.broadcasted_iota(jnp.int32, (SUBLANES, u.shape[1]), 0) == 0
    wrapped = []
    for i in range(1, taps):
        tail = u[tm - i * SUBLANES:tm - (i - 1) * SUBLANES]
        wrapped.append(jnp.where(first_sublane,
                                 pltpu.roll(carry_ref[i - 1, :, cols], 1, 0),
                                 pltpu.roll(tail, 1, 0)))
        carry_ref[i - 1, :, cols] = tail
    out = b_ref[:, cols] + w_ref[taps - 1:taps, cols] * u
    for k in range(1, taps):
        shifted = jnp.concatenate(wrapped[k - 1::-1] + [u[:tm - k * SUBLANES]], axis=0)
        out = out + w_ref[taps - 1 - k:taps - k, cols] * shifted
    return out


def _ffn_kernel(*refs, taps, with_attn, seg_in, seg_out):
    if with_attn:
        x_ref, a_ref, b_ref, wo_ref, *refs = refs
        na = a_ref.shape[1]
        h = x_ref[...] + _dot(a_ref[...], wo_ref[0:na, :]) + _dot(b_ref[...], wo_ref[na:, :])
    else:
        h, *refs = refs
    g_ref, wup_ref, cw_ref, cb_ref, wdn_ref, o_ref, stage, hseg, carry, act = refs
    F = wdn_ref.shape[0]

    @pl.when(pl.program_id(1) == 0)
    def _():
        carry[...] = jnp.zeros(carry.shape, F32)

    hs_ref = _segment_rows_ref(h, seg_in, stage, hseg)
    xn = _rms_rows(hs_ref[...], g_ref[...]).astype(BF16)
    for c in range(F // FFN_CHUNK):
        gc = slice(c * FFN_CHUNK, (c + 1) * FFN_CHUNK)
        vc = slice(F + c * FFN_CHUNK, F + (c + 1) * FFN_CHUNK)
        gg = _segment_conv(_dot(xn, wup_ref[:, gc]), carry, gc, cw_ref, cb_ref, taps)
        vv = _segment_conv(_dot(xn, wup_ref[:, vc]), carry, vc, cw_ref, cb_ref, taps)
        act[:, gc] = (_gelu(gg) * vv).astype(BF16)
    _store_segment_rows(hs_ref[...] + _dot(act[...], wdn_ref[...]), seg_out, stage, o_ref)


def _stage_shape(tm, C):
    return (C // LANES, SUBLANES * _seg_pitch(tm // SUBLANES), LANES)


def _resident(shape):
    return pl.BlockSpec(shape, lambda b, i: (0,) * len(shape), pipeline_mode=pl.Buffered(1))


SEQ_TILE = 512


def _conv_ffn(h3, g, wup, cw, cb, wdn, attn=None, seg_in=False, seg_out=False, tm=SEQ_TILE):
    assert not (seg_in and attn is not None)
    B, S, D = h3.shape
    F2 = wup.shape[1]
    taps = cw.shape[0]
    row_tile = lambda width: pl.BlockSpec((None, tm, width), lambda b, i: (b, i, 0))
    attn_args, attn_specs = (), []
    if attn is not None:
        oa, ob, wo = attn
        attn_args = (oa, ob, wo)
        attn_specs = [row_tile(oa.shape[2]), row_tile(ob.shape[2]), _resident(wo.shape)]
    return pl.pallas_call(
        functools.partial(_ffn_kernel, taps=taps, with_attn=attn is not None,
                          seg_in=seg_in, seg_out=seg_out),
        grid=(B, S // tm),
        in_specs=[
            row_tile(D),
            *attn_specs,
            _resident((1, D)),
            _resident(wup.shape),
            _resident(cw.shape),
            _resident((1, F2)),
            _resident(wdn.shape),
        ],
        out_specs=row_tile(D),
        out_shape=jax.ShapeDtypeStruct((B, S, D), F32),
        scratch_shapes=[
            pltpu.VMEM(_stage_shape(tm, D), F32),
            pltpu.VMEM((tm, D), F32),
            pltpu.VMEM((taps - 1, SUBLANES, F2), F32),
            pltpu.VMEM((tm, F2 // 2), BF16),
        ],
        compiler_params=_params(("arbitrary", "arbitrary")),
        name="conv_ffn",
    )(h3, *attn_args, g, wup, cw, cb, wdn)


def _rec_kernel(h_ref, g_ref, win_ref, cw_ref, cb_ref, wa_ref, ba_ref, wx_ref, bx_ref,
                ap_ref, wout_ref, o_ref, stage, hseg, carry, hstate, *, taps, seg_in, seg_out):
    tm, C = h_ref.shape
    seg = tm // SUBLANES
    bw = C // LRU_BLOCKS

    @pl.when(pl.program_id(1) == 0)
    def _():
        carry[...] = jnp.zeros(carry.shape, F32)
        hstate[...] = jnp.zeros(hstate.shape, F32)

    hs_ref = _segment_rows_ref(h_ref, seg_in, stage, hseg)
    xn = _rms_rows(hs_ref[...], g_ref[...]).astype(BF16)
    gate = _dot(xn, win_ref[:, 0:C])
    xr_all = _segment_conv(_dot(xn, win_ref[:, C:]), carry, slice(0, C), cw_ref, cb_ref, taps)

    ap = ap_ref[...]
    decay = -LRU_C * (jnp.maximum(-ap, 0.0) + jnp.log1p(jnp.exp(-jnp.abs(ap))))
    sublane = lax.broadcasted_iota(jnp.int32, (SUBLANES, bw), 0)
    blocks = []
    for n in range(LRU_BLOCKS):
        cols = slice(n * bw, (n + 1) * bw)
        xr = xr_all[:, cols]
        xb = xr.astype(BF16)
        r = _sigmoid(_dot(xb, wa_ref[n]) + ba_ref[:, cols])
        i = _sigmoid(_dot(xb, wx_ref[n]) + bx_ref[:, cols])
        log_a = decay[:, cols] * r
        a = jnp.exp(log_a)
        u = jnp.sqrt(-jnp.tanh(log_a) * (1.0 + a * a)) * (i * xr)

        hl = jnp.zeros((SUBLANES, bw), F32)
        pp = jnp.ones((SUBLANES, bw), F32)
        hls, pps = [], []
        for j in range(seg):
            aj = a[j * SUBLANES:(j + 1) * SUBLANES]
            hl = aj * hl + u[j * SUBLANES:(j + 1) * SUBLANES]
            pp = aj * pp
            hls.append(hl)
            pps.append(pp)
        cin = hstate[0:1, cols]
        h_in = jnp.zeros((SUBLANES, bw), F32)
        for s in range(SUBLANES):
            h_in = jnp.where(sublane == s, cin, h_in)
            cin = hl[s:s + 1] + pp[s:s + 1] * cin
        hstate[:, cols] = jnp.broadcast_to(cin, (SUBLANES, bw))
        blocks.append(jnp.concatenate([hls[j] + pps[j] * h_in for j in range(seg)], axis=0))

    hs = jnp.concatenate(blocks, axis=1)
    y = (hs * _gelu(gate)).astype(BF16)
    _store_segment_rows(hs_ref[...] + _dot(y, wout_ref[...]), seg_out, stage, o_ref)


def _recurrent_block(h3, g, win, cw, cb, wa, ba, wx, bx, ap, wout,
                     seg_in=False, seg_out=False, tm=SEQ_TILE):
    B, S, D = h3.shape
    C = wout.shape[0]
    taps = cw.shape[0]
    return pl.pallas_call(
        functools.partial(_rec_kernel, taps=taps, seg_in=seg_in, seg_out=seg_out),
        grid=(B, S // tm),
        in_specs=[
            pl.BlockSpec((None, tm, D), lambda b, i: (b, i, 0)),
            _resident((1, D)), _resident(win.shape), _resident(cw.shape), _resident((1, C)),
            _resident(wa.shape), _resident((1, C)), _resident(wx.shape), _resident((1, C)),
            _resident((1, C)), _resident(wout.shape),
        ],
        out_specs=pl.BlockSpec((None, tm, D), lambda b, i: (b, i, 0)),
        out_shape=jax.ShapeDtypeStruct((B, S, D), F32),
        scratch_shapes=[
            pltpu.VMEM(_stage_shape(tm, D), F32),
            pltpu.VMEM((tm, D), F32),
            pltpu.VMEM((taps - 1, SUBLANES, C), F32),
            pltpu.VMEM((SUBLANES, C), F32),
        ],
        compiler_params=_params(("arbitrary", "arbitrary")),
        name="recurrent_block",
    )(h3, g, win, cw, cb, wa, ba, wx, bx, ap, wout)


def _alibi_slopes(n):
    return jnp.exp2(-8.0 * jnp.arange(1, n + 1, dtype=F32) / n)


def _row(v):
    return v.reshape(1, -1).astype(F32)


def kernel(x, attn_norm, attn_w_in, attn_w_out, a_q_norm, a_k_norm, b_q_norm, b_k_norm, b_sub_norm,
           b_lam_q1, b_lam_k1, b_lam_q2, b_lam_k2, rec_norm, rec_w_in, rec_conv_w, rec_conv_b,
           rec_gate_a_w, rec_gate_a_b, rec_gate_x_w, rec_gate_x_b, rec_a_param, rec_w_out,
           ffn_norm, ffn_w_up, ffn_conv_w, ffn_conv_b, ffn_w_down):
    B, S, D = x.shape
    depth = ffn_norm.shape[0]
    slopes = _alibi_slopes(A_HEADS + B_HEADS)
    h = x
    h_is_seg = False
    for layer in range(depth):
        j = layer // 2
        attn = None
        if layer % 2 == 0:
            lam_init = 0.8 - 0.6 * math.exp(-0.3 * layer)
            scale = HEAD_DIM ** -0.5
            reps = SEC // HEAD_DIM
            ones = jnp.ones((SEC,), F32)
            head_gains = jnp.stack([
                jnp.tile(a_q_norm[j].astype(F32), reps) * (scale * LOG2E),
                jnp.tile(a_k_norm[j].astype(F32), reps), ones,
                jnp.tile(b_q_norm[j].astype(F32), reps) * (scale * LOG2E),
                jnp.tile(b_k_norm[j].astype(F32), reps), ones])
            proj = _attn_inproj(h.reshape(B * S, D), _row(attn_norm[j]),
                                attn_w_in[j].astype(BF16), head_gains)
            proj3 = proj.reshape(B, S, -1)
            oa = _dilated_attention(proj3, slopes[:A_HEADS] * LOG2E)
            lam_vecs = jnp.stack([b_lam_q1[j], b_lam_k1[j], b_lam_q2[j], b_lam_k2[j]]).astype(F32)
            ob = _diff_attention(proj3, slopes[A_HEADS:] * LOG2E, lam_vecs, _row(b_sub_norm[j]),
                                 lam_init)
            attn = (oa, ob, attn_w_out[j].astype(BF16))
        else:
            h = _recurrent_block(
                h, _row(rec_norm[j]), rec_w_in[j].astype(BF16), rec_conv_w[j].astype(F32),
                _row(rec_conv_b[j]), rec_gate_a_w[j].astype(BF16), _row(rec_gate_a_b[j]),
                rec_gate_x_w[j].astype(BF16), _row(rec_gate_x_b[j]), _row(rec_a_param[j]),
                rec_w_out[j].astype(BF16), seg_in=h_is_seg, seg_out=True)
            h_is_seg = True
        next_is_rec = layer + 1 < depth and (layer + 1) % 2 == 1
        h = _conv_ffn(h, _row(ffn_norm[layer]), ffn_w_up[layer].astype(BF16),
                      ffn_conv_w[layer].astype(F32), _row(ffn_conv_b[layer]),
                      ffn_w_down[layer].astype(BF16), attn=attn,
                      seg_in=h_is_seg, seg_out=next_is_rec)
        h_is_seg = next_is_rec
    return h
```

```python
import functools
import math

import numpy as np
import jax
import jax.numpy as jnp
from jax import lax
from jax.experimental import pallas as pl
from jax.experimental.pallas import tpu as pltpu

F32 = jnp.float32
BF16 = jnp.bfloat16

HEAD_DIM = 64
A_HEADS = 8
B_HEADS = 4
DILATED_CONFIGS = ((128, 1), (512, 4), (2048, 16))
BAND = 128
MAX_WINDOW = 2048
LRU_BLOCKS = 8
LRU_C = 8.0
NORM_EPS = 1e-6
NEG_INF = -1e30
LOG2E = math.log2(math.e)
LANES = 128
SUBLANES = 8
VMEM_LIMIT = 56 * 1024 * 1024


def _gelu(x):
    c = math.sqrt(2.0 / math.pi)
    return x * (0.5 * (1.0 + jnp.tanh(c * (x + 0.044715 * (x * x * x)))))


def _sigmoid(x):
    return 1.0 / (1.0 + jnp.exp(-x))


def _rms_rows(x, g):
    ms = jnp.mean(x * x, axis=-1, keepdims=True)
    return x * lax.rsqrt(ms + NORM_EPS) * g


def _dot(a, b):
    return jnp.dot(a, b, preferred_element_type=F32)


def _dot_nt(a, b):
    return lax.dot_general(a, b, (((1,), (1,)), ((), ())), preferred_element_type=F32)


def _params(sem):
    return pltpu.CompilerParams(dimension_semantics=sem, vmem_limit_bytes=VMEM_LIMIT)


SEC = 512
N_SEC = 6


def _inproj_kernel(x_ref, g_ref, w_ref, hg_ref, p_ref, o_ref):
    xn = _rms_rows(x_ref[...], g_ref[...]).astype(BF16)
    for s in range(N_SEC):
        y = _dot(xn, w_ref[:, s * SEC:(s + 1) * SEC])
        if s % 3 == 2:
            o_ref[:, s * SEC:(s + 1) * SEC] = y.astype(BF16)
        else:
            ms = _dot((y * y).astype(BF16), p_ref[...])
            o_ref[:, s * SEC:(s + 1) * SEC] = (
                y * lax.rsqrt(ms + NORM_EPS) * hg_ref[s:s + 1, :]).astype(BF16)


def _attn_inproj(x2, g, w, head_gains, tm=512):
    T, D = x2.shape
    N = w.shape[1]
    blk = np.kron(np.eye(SEC // HEAD_DIM), np.full((HEAD_DIM, HEAD_DIM), 1.0 / HEAD_DIM))
    pmat = jnp.asarray(blk, dtype=BF16)
    return pl.pallas_call(
        _inproj_kernel,
        grid=(T // tm,),
        in_specs=[
            pl.BlockSpec((tm, D), lambda i: (i, 0)),
            pl.BlockSpec((1, D), lambda i: (0, 0)),
            pl.BlockSpec((D, N), lambda i: (0, 0)),
            pl.BlockSpec((N_SEC, SEC), lambda i: (0, 0)),
            pl.BlockSpec((SEC, SEC), lambda i: (0, 0)),
        ],
        out_specs=pl.BlockSpec((tm, N), lambda i: (i, 0)),
        out_shape=jax.ShapeDtypeStruct((T, N), BF16),
        compiler_params=_params(("parallel",)),
        name="attn_inproj",
    )(x2, g, w, head_gains, pmat)


ROW_GROUP = 16
ROW_PITCH = 24


def _spread_row(t):
    return (t // ROW_GROUP) * ROW_PITCH + t % ROW_GROUP


def _dilated_kernel(slopes_ref, q_ref, k_ref, v_ref, o_ref, qf, kf, vf, qp, kp, vp, ob, lb, bias_buf):
    S = q_ref.shape[0]
    pad = kf.shape[0] - S
    g = pl.program_id(1)
    qf[...] = q_ref[...].astype(F32)
    kf[0:pad, :] = jnp.zeros((pad, LANES), F32)
    vf[0:pad, :] = jnp.zeros((pad, LANES), F32)
    kf[pad:, :] = k_ref[...].astype(F32)
    vf[pad:, :] = v_ref[...].astype(F32)
    kp[0:_spread_row(pad), :] = jnp.zeros((_spread_row(pad), LANES), F32)
    vp[0:_spread_row(pad), :] = jnp.zeros((_spread_row(pad), LANES), F32)

    def spread(grp, carry):
        src = pl.ds(pl.multiple_of(grp * ROW_GROUP, ROW_GROUP), ROW_GROUP)
        dst = pl.ds(pl.multiple_of(grp * ROW_PITCH, SUBLANES), ROW_GROUP)
        dst_kv = pl.ds(pl.multiple_of(grp * ROW_PITCH + _spread_row(pad), SUBLANES), ROW_GROUP)
        qp[dst, :] = q_ref[src, :].astype(F32)
        kp[dst_kv, :] = k_ref[src, :].astype(F32)
        vp[dst_kv, :] = v_ref[src, :].astype(F32)
        return carry
    lax.fori_loop(0, S // ROW_GROUP, spread, 0, unroll=8)

    lo = lax.broadcasted_iota(jnp.int32, (BAND, LANES), 1) < HEAD_DIM
    ii = lax.broadcasted_iota(jnp.int32, (BAND, 2 * BAND), 0)
    jj = lax.broadcasted_iota(jnp.int32, (BAND, 2 * BAND), 1)
    delta = ii + BAND - jj
    in_band = (delta >= 0) & (delta <= BAND)
    prev_half = lax.broadcasted_iota(jnp.int32, (2 * BAND, 2 * BAND), 1) < BAND
    sl0 = slopes_ref[2 * g]
    sl1 = slopes_ref[2 * g + 1]
    ones = jnp.ones((2 * BAND, LANES), BF16)

    for bi, (window, d) in enumerate(DILATED_CONFIGS):
        span = BAND * d
        nbs = S // span
        dist = (delta * d).astype(F32)
        bias = jnp.concatenate(
            [jnp.where(in_band, -sl0 * dist, NEG_INF),
             jnp.where(in_band, -sl1 * dist, NEG_INF)], axis=0)
        bias_buf[2 * bi] = bias
        bias_buf[2 * bi + 1] = jnp.where(prev_half, NEG_INF, bias)

        def body(blk, carry, d=d, span=span, nbs=nbs, bi=bi):
            r = blk // nbs
            nb = blk - r * nbs
            start = nb * span + r
            if d % ROW_GROUP == 0:
                stride = d // ROW_GROUP * ROW_PITCH
                q_row = nb * _spread_row(span) + r
                kv_row = q_row + _spread_row(pad) - _spread_row(span)
                q = qp[pl.ds(q_row, BAND, stride=stride), :].astype(BF16)
                k = kp[pl.ds(kv_row, 2 * BAND, stride=stride), :].astype(BF16)
                v = vp[pl.ds(kv_row, 2 * BAND, stride=stride), :].astype(BF16)
            else:
                q = qf[pl.ds(start, BAND, stride=d), :].astype(BF16)
                k = kf[pl.ds(pad + start - span, 2 * BAND, stride=d), :].astype(BF16)
                v = vf[pl.ds(pad + start - span, 2 * BAND, stride=d), :].astype(BF16)
            zero = jnp.zeros_like(q)
            q2 = jnp.concatenate([jnp.where(lo, q, zero), jnp.where(lo, zero, q)], axis=0)
            s = _dot_nt(q2, k) + bias_buf[2 * bi + jnp.where(nb == 0, 1, 0)]
            m = jnp.max(s, axis=1, keepdims=True)
            e = jnp.exp2(s - m)
            pv = _dot(e.astype(BF16), jnp.concatenate([v, ones], axis=1))
            den = pv[:, LANES:]
            o2 = pv[:, :LANES] / den
            lse = m + jnp.log2(den)
            ob[bi, pl.ds(start, BAND, stride=d), :] = jnp.where(lo, o2[:BAND], o2[BAND:])
            lb[bi, pl.ds(start, BAND, stride=d), :] = jnp.where(lo, lse[:BAND], lse[BAND:])
            return carry

        lax.fori_loop(0, S // BAND, body, 0, unroll=8)

    rows = 512
    def mix(c, carry):
        sl = pl.ds(pl.multiple_of(c * rows, rows), rows)
        l0, l1, l2 = lb[0, sl, :], lb[1, sl, :], lb[2, sl, :]
        m = jnp.maximum(jnp.maximum(l0, l1), l2)
        w0, w1, w2 = jnp.exp2(l0 - m), jnp.exp2(l1 - m), jnp.exp2(l2 - m)
        tot = w0 + w1 + w2
        o = (w0 * ob[0, sl, :] + w1 * ob[1, sl, :] + w2 * ob[2, sl, :]) / tot
        o_ref[sl, :] = o.astype(o_ref.dtype)
        return carry
    lax.fori_loop(0, S // rows, mix, 0)


def _dilated_attention(proj3, slopes):
    B, S, _ = proj3.shape
    assert S % MAX_WINDOW == 0
    pairs = A_HEADS // 2
    blk = lambda off: pl.BlockSpec((None, S, LANES), lambda b, g, off=off: (b, 0, off + g))
    return pl.pallas_call(
        _dilated_kernel,
        grid=(B, pairs),
        in_specs=[
            pl.BlockSpec(memory_space=pltpu.SMEM),
            blk(0), blk(pairs), blk(2 * pairs),
        ],
        out_specs=pl.BlockSpec((None, S, LANES), lambda b, g: (b, 0, g)),
        out_shape=jax.ShapeDtypeStruct((B, S, A_HEADS * HEAD_DIM), BF16),
        scratch_shapes=[
            pltpu.VMEM((S, LANES), F32),
            pltpu.VMEM((MAX_WINDOW + S, LANES), F32),
            pltpu.VMEM((MAX_WINDOW + S, LANES), F32),
            pltpu.VMEM((_spread_row(S), LANES), F32),
            pltpu.VMEM((_spread_row(MAX_WINDOW + S), LANES), F32),
            pltpu.VMEM((_spread_row(MAX_WINDOW + S), LANES), F32),
            pltpu.VMEM((3, S, LANES), F32),
            pltpu.VMEM((3, S, LANES), F32),
            pltpu.VMEM((2 * len(DILATED_CONFIGS), 2 * BAND, 2 * BAND), F32),
        ],
        compiler_params=_params(("parallel", "parallel")),
        name="dilated_attn",
    )(slopes, proj3, proj3, proj3)


TQ = 512
KV_UNROLL = 2
DIFF_HEADS = 4


def _diff_kernel(slopes_ref, q_ref, k_ref, v_ref, lam_ref, sg_ref, o_ref,
                 m_ref, acc_ref, *, lam_init):
    hg = pl.program_id(1)
    qi = pl.program_id(2)
    lo = lax.broadcasted_iota(jnp.int32, (TQ, LANES), 1) < HEAD_DIM
    kidx = lax.broadcasted_iota(jnp.int32, (1, TQ), 1).astype(F32)
    ones = jnp.ones((TQ, LANES), BF16)
    heads = range(DIFF_HEADS)
    cols = [slice(hh * LANES, (hh + 1) * LANES) for hh in heads]
    slopes = [slopes_ref[hg * DIFF_HEADS + hh] for hh in heads]
    q2 = []
    for hh in heads:
        q = q_ref[:, cols[hh]]
        zero = jnp.zeros_like(q)
        q2.append(jnp.concatenate([jnp.where(lo, q, zero), jnp.where(lo, zero, q)], axis=0))

    m_ref[...] = jnp.full(m_ref.shape, NEG_INF, F32)
    acc_ref[...] = jnp.zeros(acc_ref.shape, F32)

    def step(j, masked, hh):
        ks = pl.ds(pl.multiple_of(j * TQ, TQ), TQ)
        k = k_ref[ks, cols[hh]]
        v1 = jnp.concatenate([v_ref[ks, cols[hh]], ones], axis=1)
        s = _dot_nt(q2[hh], k) + slopes[hh] * (kidx + (j * TQ).astype(F32))
        if masked:
            rel = (lax.broadcasted_iota(jnp.int32, (2 * TQ, TQ), 0) % TQ
                   - lax.broadcasted_iota(jnp.int32, (2 * TQ, TQ), 1))
            s = jnp.where(rel >= 0, s, NEG_INF)
        m_old = m_ref[hh]
        m_new = jnp.maximum(m_old, jnp.max(s, axis=1, keepdims=True))
        alpha = jnp.exp2(m_old - m_new)
        e = jnp.exp2(s - jnp.concatenate([m_new] * (TQ // LANES), axis=1))
        acc_ref[hh] = (jnp.concatenate([alpha, alpha], axis=1) * acc_ref[hh]
                       + _dot(e.astype(BF16), v1))
        m_ref[hh] = m_new

    def body(jq, carry):
        for u in range(KV_UNROLL):
            for hh in heads:
                step(KV_UNROLL * jq + u, False, hh)
        return carry
    nq = qi // KV_UNROLL
    lax.fori_loop(0, nq, body, 0)

    rem = qi - nq * KV_UNROLL
    for r in range(KV_UNROLL):
        @pl.when(rem == r)
        def _(r=r):
            for u in range(r):
                for hh in heads:
                    step(qi - r + u, False, hh)
            for hh in heads:
                step(qi, True, hh)

    lq = lam_ref[...]
    lam = (jnp.exp(jnp.sum(lq[0:1] * lq[1:2], axis=1, keepdims=True))
           - jnp.exp(jnp.sum(lq[2:3] * lq[3:4], axis=1, keepdims=True)) + lam_init)
    for hh in heads:
        on = acc_ref[hh, :, 0:LANES] / acc_ref[hh, :, LANES:]
        o = on[:TQ] - lam * on[TQ:]
        o = _rms_rows(o, sg_ref[...]) * (1.0 - lam_init)
        o_ref[:, cols[hh]] = o.astype(o_ref.dtype)


def _diff_attention(proj3, slopes, lam_vecs, sub_gain, lam_init):
    B, S, _ = proj3.shape
    width = DIFF_HEADS * LANES
    groups = B_HEADS // DIFF_HEADS
    qoff = 3 * A_HEADS * HEAD_DIM // width
    koff = qoff + groups
    voff = koff + groups
    return pl.pallas_call(
        functools.partial(_diff_kernel, lam_init=lam_init),
        grid=(B, groups, S // TQ),
        in_specs=[
            pl.BlockSpec(memory_space=pltpu.SMEM),
            pl.BlockSpec((None, TQ, width), lambda b, h, i: (b, i, qoff + h)),
            pl.BlockSpec((None, S, width), lambda b, h, i: (b, 0, koff + h)),
            pl.BlockSpec((None, S, width), lambda b, h, i: (b, 0, voff + h)),
            pl.BlockSpec((4, HEAD_DIM), lambda b, h, i: (0, 0)),
            pl.BlockSpec((1, 2 * HEAD_DIM), lambda b, h, i: (0, 0)),
        ],
        out_specs=pl.BlockSpec((None, TQ, width), lambda b, h, i: (b, i, h)),
        out_shape=jax.ShapeDtypeStruct((B, S, B_HEADS * 2 * HEAD_DIM), BF16),
        scratch_shapes=[
            pltpu.VMEM((DIFF_HEADS, 2 * TQ, LANES), F32),
            pltpu.VMEM((DIFF_HEADS, 2 * TQ, 2 * LANES), F32),
        ],
        compiler_params=_params(("parallel", "parallel", "parallel")),
        name="diff_attn",
    )(slopes, proj3, proj3, proj3, lam_vecs, sub_gain)


FFN_CHUNK = 512


def _seg_pitch(seg):
    p = seg // SUBLANES + 1
    return SUBLANES * (p if p % 2 else p + 1)


def _to_segment_rows(src, stage, dst_ref):
    tm, C = src.shape
    seg = tm // SUBLANES
    pitch = _seg_pitch(seg)
    for n in range(C // LANES):
        cols = slice(n * LANES, (n + 1) * LANES)
        for s in range(SUBLANES):
            stage[n, s * pitch:s * pitch + seg, :] = src[s * seg:(s + 1) * seg, cols]
    for j in range(seg):
        dst_ref[j * SUBLANES:(j + 1) * SUBLANES, :] = jnp.concatenate(
            [stage[n, pl.ds(j, SUBLANES, stride=pitch), :] for n in range(C // LANES)], axis=1)


def _from_segment_rows(val, stage, dst_ref):
    tm, C = val.shape
    seg = tm // SUBLANES
    pitch = _seg_pitch(seg)
    for n in range(C // LANES):
        cols = slice(n * LANES, (n + 1) * LANES)
        for j in range(seg):
            stage[n, pl.ds(j, SUBLANES, stride=pitch), :] = val[j * SUBLANES:(j + 1) * SUBLANES, cols]
        for s in range(SUBLANES):
            dst_ref[s * seg:(s + 1) * seg, cols] = stage[n, s * pitch:s * pitch + seg, :]


def _segment_rows_ref(src, seg_in, stage, hseg):
    if seg_in:
        return src
    _to_segment_rows(src, stage, hseg)
    return hseg


def _store_segment_rows(val, seg_out, stage, o_ref):
    if seg_out:
        o_ref[...] = val
    else:
        _from_segment_rows(val, stage, o_ref)


def _segment_conv(u, carry_ref, cols, w_ref, b_ref, taps):
    tm = u.shape[0]
    first_sublane = lax.broadcasted_iota(jnp.int32, (SUBLANES, u.shape[1]), 0) == 0
    wrapped = []
    for i in range(1, taps):
        tail = u[tm - i * SUBLANES:tm - (i - 1) * SUBLANES]
        wrapped.append(jnp.where(first_sublane,
                                 pltpu.roll(carry_ref[i - 1, :, cols], 1, 0),
                                 pltpu.roll(tail, 1, 0)))
        carry_ref[i - 1, :, cols] = tail
    out = b_ref[:, cols] + w_ref[taps - 1:taps, cols] * u
    for k in range(1, taps):
        shifted = jnp.concatenate(wrapped[k - 1::-1] + [u[:tm - k * SUBLANES]], axis=0)
        out = out + w_ref[taps - 1 - k:taps - k, cols] * shifted
    return out


def _ffn_kernel(*refs, taps, with_attn, seg_in, seg_out):
    if with_attn:
        x_ref, a_ref, b_ref, wo_ref, *refs = refs
        na = a_ref.shape[1]
        h = x_ref[...] + _dot(a_ref[...], wo_ref[0:na, :]) + _dot(b_ref[...], wo_ref[na:, :])
    else:
        h, *refs = refs
    g_ref, wup_ref, cw_ref, cb_ref, wdn_ref, o_ref, stage, hseg, carry, act = refs
    F = wdn_ref.shape[0]

    @pl.when(pl.program_id(1) == 0)
    def _():
        carry[...] = jnp.zeros(carry.shape, F32)

    hs_ref = _segment_rows_ref(h, seg_in, stage, hseg)
    xn = _rms_rows(hs_ref[...], g_ref[...]).astype(BF16)
    for c in range(F // FFN_CHUNK):
        gc = slice(c * FFN_CHUNK, (c + 1) * FFN_CHUNK)
        vc = slice(F + c * FFN_CHUNK, F + (c + 1) * FFN_CHUNK)
        gg = _segment_conv(_dot(xn, wup_ref[:, gc]), carry, gc, cw_ref, cb_ref, taps)
        vv = _segment_conv(_dot(xn, wup_ref[:, vc]), carry, vc, cw_ref, cb_ref, taps)
        act[:, gc] = (_gelu(gg) * vv).astype(BF16)
    _store_segment_rows(hs_ref[...] + _dot(act[...], wdn_ref[...]), seg_out, stage, o_ref)


def _stage_shape(tm, C):
    return (C // LANES, SUBLANES * _seg_pitch(tm // SUBLANES), LANES)


def _resident(shape):
    return pl.BlockSpec(shape, lambda b, i: (0,) * len(shape), pipeline_mode=pl.Buffered(1))


SEQ_TILE = 512


def _conv_ffn(h3, g, wup, cw, cb, wdn, attn=None, seg_in=False, seg_out=False, tm=SEQ_TILE):
    assert not (seg_in and attn is not None)
    B, S, D = h3.shape
    F2 = wup.shape[1]
    taps = cw.shape[0]
    row_tile = lambda width: pl.BlockSpec((None, tm, width), lambda b, i: (b, i, 0))
    attn_args, attn_specs = (), []
    if attn is not None:
        oa, ob, wo = attn
        attn_args = (oa, ob, wo)
        attn_specs = [row_tile(oa.shape[2]), row_tile(ob.shape[2]), _resident(wo.shape)]
    return pl.pallas_call(
        functools.partial(_ffn_kernel, taps=taps, with_attn=attn is not None,
                          seg_in=seg_in, seg_out=seg_out),
        grid=(B, S // tm),
        in_specs=[
            row_tile(D),
            *attn_specs,
            _resident((1, D)),
            _resident(wup.shape),
            _resident(cw.shape),
            _resident((1, F2)),
            _resident(wdn.shape),
        ],
        out_specs=row_tile(D),
        out_shape=jax.ShapeDtypeStruct((B, S, D), F32),
        scratch_shapes=[
            pltpu.VMEM(_stage_shape(tm, D), F32),
            pltpu.VMEM((tm, D), F32),
            pltpu.VMEM((taps - 1, SUBLANES, F2), F32),
            pltpu.VMEM((tm, F2 // 2), BF16),
        ],
        compiler_params=_params(("arbitrary", "arbitrary")),
        name="conv_ffn",
    )(h3, *attn_args, g, wup, cw, cb, wdn)


def _rec_kernel(h_ref, g_ref, win_ref, cw_ref, cb_ref, wa_ref, ba_ref, wx_ref, bx_ref,
                ap_ref, wout_ref, o_ref, stage, hseg, carry, hstate, *, taps, seg_in, seg_out):
    tm, C = h_ref.shape
    seg = tm // SUBLANES
    bw = C // LRU_BLOCKS

    @pl.when(pl.program_id(1) == 0)
    def _():
        carry[...] = jnp.zeros(carry.shape, F32)
        hstate[...] = jnp.zeros(hstate.shape, F32)

    hs_ref = _segment_rows_ref(h_ref, seg_in, stage, hseg)
    xn = _rms_rows(hs_ref[...], g_ref[...]).astype(BF16)
    gate = _dot(xn, win_ref[:, 0:C])
    xr_all = _segment_conv(_dot(xn, win_ref[:, C:]), carry, slice(0, C), cw_ref, cb_ref, taps)

    ap = ap_ref[...]
    decay = -LRU_C * (jnp.maximum(-ap, 0.0) + jnp.log1p(jnp.exp(-jnp.abs(ap))))
    sublane = lax.broadcasted_iota(jnp.int32, (SUBLANES, bw), 0)
    blocks = []
    for n in range(LRU_BLOCKS):
        cols = slice(n * bw, (n + 1) * bw)
        xr = xr_all[:, cols]
        xb = xr.astype(BF16)
        r = _sigmoid(_dot(xb, wa_ref[n]) + ba_ref[:, cols])
        i = _sigmoid(_dot(xb, wx_ref[n]) + bx_ref[:, cols])
        log_a = decay[:, cols] * r
        a = jnp.exp(log_a)
        u = jnp.sqrt(-jnp.tanh(log_a) * (1.0 + a * a)) * (i * xr)

        hl = jnp.zeros((SUBLANES, bw), F32)
        pp = jnp.ones((SUBLANES, bw), F32)
        hls, pps = [], []
        for j in range(seg):
            aj = a[j * SUBLANES:(j + 1) * SUBLANES]
            hl = aj * hl + u[j * SUBLANES:(j + 1) * SUBLANES]
            pp = aj * pp
            hls.append(hl)
            pps.append(pp)
        cin = hstate[0:1, cols]
        h_in = jnp.zeros((SUBLANES, bw), F32)
        for s in range(SUBLANES):
            h_in = jnp.where(sublane == s, cin, h_in)
            cin = hl[s:s + 1] + pp[s:s + 1] * cin
        hstate[:, cols] = jnp.broadcast_to(cin, (SUBLANES, bw))
        blocks.append(jnp.concatenate([hls[j] + pps[j] * h_in for j in range(seg)], axis=0))

    hs = jnp.concatenate(blocks, axis=1)
    y = (hs * _gelu(gate)).astype(BF16)
    _store_segment_rows(hs_ref[...] + _dot(y, wout_ref[...]), seg_out, stage, o_ref)


def _recurrent_block(h3, g, win, cw, cb, wa, ba, wx, bx, ap, wout,
                     seg_in=False, seg_out=False, tm=SEQ_TILE):
    B, S, D = h3.shape
    C = wout.shape[0]
    taps = cw.shape[0]
    return pl.pallas_call(
        functools.partial(_rec_kernel, taps=taps, seg_in=seg_in, seg_out=seg_out),
        grid=(B, S // tm),
        in_specs=[
            pl.BlockSpec((None, tm, D), lambda b, i: (b, i, 0)),
            _resident((1, D)), _resident(win.shape), _resident(cw.shape), _resident((1, C)),
            _resident(wa.shape), _resident((1, C)), _resident(wx.shape), _resident((1, C)),
            _resident((1, C)), _resident(wout.shape),
        ],
        out_specs=pl.BlockSpec((None, tm, D), lambda b, i: (b, i, 0)),
        out_shape=jax.ShapeDtypeStruct((B, S, D), F32),
        scratch_shapes=[
            pltpu.VMEM(_stage_shape(tm, D), F32),
            pltpu.VMEM((tm, D), F32),
            pltpu.VMEM((taps - 1, SUBLANES, C), F32),
            pltpu.VMEM((SUBLANES, C), F32),
        ],
        compiler_params=_params(("arbitrary", "arbitrary")),
        name="recurrent_block",
    )(h3, g, win, cw, cb, wa, ba, wx, bx, ap, wout)


def _alibi_slopes(n):
    return jnp.exp2(-8.0 * jnp.arange(1, n + 1, dtype=F32) / n)


def _row(v):
    return v.reshape(1, -1).astype(F32)


def kernel(x, attn_norm, attn_w_in, attn_w_out, a_q_norm, a_k_norm, b_q_norm, b_k_norm, b_sub_norm,
           b_lam_q1, b_lam_k1, b_lam_q2, b_lam_k2, rec_norm, rec_w_in, rec_conv_w, rec_conv_b,
           rec_gate_a_w, rec_gate_a_b, rec_gate_x_w, rec_gate_x_b, rec_a_param, rec_w_out,
           ffn_norm, ffn_w_up, ffn_conv_w, ffn_conv_b, ffn_w_down):
    B, S, D = x.shape
    depth = ffn_norm.shape[0]
    slopes = _alibi_slopes(A_HEADS + B_HEADS)
    h = x
    h_is_seg = False
    for layer in range(depth):
        j = layer // 2
        attn = None
        if layer % 2 == 0:
            lam_init = 0.8 - 0.6 * math.exp(-0.3 * layer)
            scale = HEAD_DIM ** -0.5
            reps = SEC // HEAD_DIM
            ones = jnp.ones((SEC,), F32)
            head_gains = jnp.stack([
                jnp.tile(a_q_norm[j].astype(F32), reps) * (scale * LOG2E),
                jnp.tile(a_k_norm[j].astype(F32), reps), ones,
                jnp.tile(b_q_norm[j].astype(F32), reps) * (scale * LOG2E),
                jnp.tile(b_k_norm[j].astype(F32), reps), ones])
            proj = _attn_inproj(h.reshape(B * S, D), _row(attn_norm[j]),
                                attn_w_in[j].astype(BF16), head_gains)
            proj3 = proj.reshape(B, S, -1)
            oa = _dilated_attention(proj3, slopes[:A_HEADS] * LOG2E)
            lam_vecs = jnp.stack([b_lam_q1[j], b_lam_k1[j], b_lam_q2[j], b_lam_k2[j]]).astype(F32)
            ob = _diff_attention(proj3, slopes[A_HEADS:] * LOG2E, lam_vecs, _row(b_sub_norm[j]),
                                 lam_init)
            attn = (oa, ob, attn_w_out[j].astype(BF16))
        else:
            h = _recurrent_block(
                h, _row(rec_norm[j]), rec_w_in[j].astype(BF16), rec_conv_w[j].astype(F32),
                _row(rec_conv_b[j]), rec_gate_a_w[j].astype(BF16), _row(rec_gate_a_b[j]),
                rec_gate_x_w[j].astype(BF16), _row(rec_gate_x_b[j]), _row(rec_a_param[j]),
                rec_w_out[j].astype(BF16), seg_in=h_is_seg, seg_out=True)
            h_is_seg = True
        next_is_rec = layer + 1 < depth and (layer + 1) % 2 == 1
        h = _conv_ffn(h, _row(ffn_norm[layer]), ffn_w_up[layer].astype(BF16),
                      ffn_conv_w[layer].astype(F32), _row(ffn_conv_b[layer]),
                      ffn_w_down[layer].astype(BF16), attn=attn,
                      seg_in=h_is_seg, seg_out=next_is_rec)
        h_is_seg = next_is_rec
    return h
```

```python
import functools
import math

import numpy as np
import jax
import jax.numpy as jnp
from jax import lax
from jax.experimental import pallas as pl
from jax.experimental.pallas import tpu as pltpu

F32 = jnp.float32
BF16 = jnp.bfloat16

HEAD_DIM = 64
A_HEADS = 8
B_HEADS = 4
DILATED_CONFIGS = ((128, 1), (512, 4), (2048, 16))
BAND = 128
MAX_WINDOW = 2048
LRU_BLOCKS = 8
LRU_C = 8.0
NORM_EPS = 1e-6
NEG_INF = -1e30
LOG2E = math.log2(math.e)
LANES = 128
SUBLANES = 8
VMEM_LIMIT = 56 * 1024 * 1024


def _gelu(x):
    c = math.sqrt(2.0 / math.pi)
    return x * (0.5 * (1.0 + jnp.tanh(c * (x + 0.044715 * (x * x * x)))))


def _sigmoid(x):
    return 1.0 / (1.0 + jnp.exp(-x))


def _rms_rows(x, g):
    ms = jnp.mean(x * x, axis=-1, keepdims=True)
    return x * lax.rsqrt(ms + NORM_EPS) * g


def _dot(a, b):
    return jnp.dot(a, b, preferred_element_type=F32)


def _dot_nt(a, b):
    return lax.dot_general(a, b, (((1,), (1,)), ((), ())), preferred_element_type=F32)


def _params(sem):
    return pltpu.CompilerParams(dimension_semantics=sem, vmem_limit_bytes=VMEM_LIMIT)


SEC = 512
N_SEC = 6
MXU_TILE = 256


def _inproj_kernel(x_ref, g_ref, w_ref, hg_ref, p_ref, o_ref):
    xn = _rms_rows(x_ref[...], g_ref[...]).astype(BF16)
    for s in range(N_SEC):
        y = _dot(xn, w_ref[:, s * SEC:(s + 1) * SEC])
        if s % 3 == 2:
            o_ref[:, s * SEC:(s + 1) * SEC] = y.astype(BF16)
        else:
            y2 = (y * y).astype(BF16)
            ms = jnp.concatenate(
                [_dot(y2[:, c:c + MXU_TILE], p_ref[...]) for c in range(0, SEC, MXU_TILE)], axis=1)
            o_ref[:, s * SEC:(s + 1) * SEC] = (
                y * lax.rsqrt(ms + NORM_EPS) * hg_ref[s:s + 1, :]).astype(BF16)


def _attn_inproj(x2, g, w, head_gains, tm=512):
    T, D = x2.shape
    N = w.shape[1]
    blk = np.kron(np.eye(MXU_TILE // HEAD_DIM), np.full((HEAD_DIM, HEAD_DIM), 1.0 / HEAD_DIM))
    pmat = jnp.asarray(blk, dtype=BF16)
    return pl.pallas_call(
        _inproj_kernel,
        grid=(T // tm,),
        in_specs=[
            pl.BlockSpec((tm, D), lambda i: (i, 0)),
            pl.BlockSpec((1, D), lambda i: (0, 0)),
            pl.BlockSpec((D, N), lambda i: (0, 0)),
            pl.BlockSpec((N_SEC, SEC), lambda i: (0, 0)),
            pl.BlockSpec((MXU_TILE, MXU_TILE), lambda i: (0, 0)),
        ],
        out_specs=pl.BlockSpec((tm, N), lambda i: (i, 0)),
        out_shape=jax.ShapeDtypeStruct((T, N), BF16),
        compiler_params=_params(("parallel",)),
        name="attn_inproj",
    )(x2, g, w, head_gains, pmat)


ROW_GROUP = 16
ROW_PITCH = 24


def _spread_row(t):
    return (t // ROW_GROUP) * ROW_PITCH + t % ROW_GROUP


def _dilated_kernel(slopes_ref, q_ref, k_ref, v_ref, o_ref, qf, kf, vf, qp, kp, vp, ob, lb, bias_buf):
    S = q_ref.shape[0]
    pad = kf.shape[0] - S
    g = pl.program_id(1)
    qf[...] = q_ref[...].astype(F32)
    kf[0:pad, :] = jnp.zeros((pad, LANES), F32)
    vf[0:pad, :] = jnp.zeros((pad, LANES), F32)
    kf[pad:, :] = k_ref[...].astype(F32)
    vf[pad:, :] = v_ref[...].astype(F32)
    kp[0:_spread_row(pad), :] = jnp.zeros((_spread_row(pad), LANES), F32)
    vp[0:_spread_row(pad), :] = jnp.zeros((_spread_row(pad), LANES), F32)

    def spread(grp, carry):
        src = pl.ds(pl.multiple_of(grp * ROW_GROUP, ROW_GROUP), ROW_GROUP)
        dst = pl.ds(pl.multiple_of(grp * ROW_PITCH, SUBLANES), ROW_GROUP)
        dst_kv = pl.ds(pl.multiple_of(grp * ROW_PITCH + _spread_row(pad), SUBLANES), ROW_GROUP)
        qp[dst, :] = q_ref[src, :].astype(F32)
        kp[dst_kv, :] = k_ref[src, :].astype(F32)
        vp[dst_kv, :] = v_ref[src, :].astype(F32)
        return carry
    lax.fori_loop(0, S // ROW_GROUP, spread, 0, unroll=8)

    lo = lax.broadcasted_iota(jnp.int32, (BAND, LANES), 1) < HEAD_DIM
    ii = lax.broadcasted_iota(jnp.int32, (BAND, 2 * BAND), 0)
    jj = lax.broadcasted_iota(jnp.int32, (BAND, 2 * BAND), 1)
    delta = ii + BAND - jj
    in_band = (delta >= 0) & (delta <= BAND)
    prev_half = lax.broadcasted_iota(jnp.int32, (2 * BAND, 2 * BAND), 1) < BAND
    sl0 = slopes_ref[2 * g]
    sl1 = slopes_ref[2 * g + 1]
    ones = jnp.ones((2 * BAND, LANES), BF16)

    for bi, (window, d) in enumerate(DILATED_CONFIGS):
        span = BAND * d
        nbs = S // span
        dist = (delta * d).astype(F32)
        bias = jnp.concatenate(
            [jnp.where(in_band, -sl0 * dist, NEG_INF),
             jnp.where(in_band, -sl1 * dist, NEG_INF)], axis=0)
        bias_buf[2 * bi] = bias
        bias_buf[2 * bi + 1] = jnp.where(prev_half, NEG_INF, bias)

        def body(blk, carry, d=d, span=span, nbs=nbs, bi=bi):
            r = blk // nbs
            nb = blk - r * nbs
            start = nb * span + r
            if d % ROW_GROUP == 0:
                stride = d // ROW_GROUP * ROW_PITCH
                q_row = nb * _spread_row(span) + r
                kv_row = q_row + _spread_row(pad) - _spread_row(span)
                q = qp[pl.ds(q_row, BAND, stride=stride), :].astype(BF16)
                k = kp[pl.ds(kv_row, 2 * BAND, stride=stride), :].astype(BF16)
                v = vp[pl.ds(kv_row, 2 * BAND, stride=stride), :].astype(BF16)
            else:
                q = qf[pl.ds(start, BAND, stride=d), :].astype(BF16)
                k = kf[pl.ds(pad + start - span, 2 * BAND, stride=d), :].astype(BF16)
                v = vf[pl.ds(pad + start - span, 2 * BAND, stride=d), :].astype(BF16)
            zero = jnp.zeros_like(q)
            q2 = jnp.concatenate([jnp.where(lo, q, zero), jnp.where(lo, zero, q)], axis=0)
            s = _dot_nt(q2, k) + bias_buf[2 * bi + jnp.where(nb == 0, 1, 0)]
            m = jnp.max(s, axis=1, keepdims=True)
            e = jnp.exp2(s - m)
            pv = _dot(e.astype(BF16), jnp.concatenate([v, ones], axis=1))
            den = pv[:, LANES:]
            o2 = pv[:, :LANES] / den
            lse = m + jnp.log2(den)
            ob[bi, pl.ds(start, BAND, stride=d), :] = jnp.where(lo, o2[:BAND], o2[BAND:])
            lb[bi, pl.ds(start, BAND, stride=d), :] = jnp.where(lo, lse[:BAND], lse[BAND:])
            return carry

        lax.fori_loop(0, S // BAND, body, 0, unroll=8)

    rows = 512
    def mix(c, carry):
        sl = pl.ds(pl.multiple_of(c * rows, rows), rows)
        l0, l1, l2 = lb[0, sl, :], lb[1, sl, :], lb[2, sl, :]
        m = jnp.maximum(jnp.maximum(l0, l1), l2)
        w0, w1, w2 = jnp.exp2(l0 - m), jnp.exp2(l1 - m), jnp.exp2(l2 - m)
        tot = w0 + w1 + w2
        o = (w0 * ob[0, sl, :] + w1 * ob[1, sl, :] + w2 * ob[2, sl, :]) / tot
        o_ref[sl, :] = o.astype(o_ref.dtype)
        return carry
    lax.fori_loop(0, S // rows, mix, 0)


def _dilated_attention(proj3, slopes):
    B, S, _ = proj3.shape
    assert S % MAX_WINDOW == 0
    pairs = A_HEADS // 2
    blk = lambda off: pl.BlockSpec((None, S, LANES), lambda b, g, off=off: (b, 0, off + g))
    return pl.pallas_call(
        _dilated_kernel,
        grid=(B, pairs),
        in_specs=[
            pl.BlockSpec(memory_space=pltpu.SMEM),
            blk(0), blk(pairs), blk(2 * pairs),
        ],
        out_specs=pl.BlockSpec((None, S, LANES), lambda b, g: (b, 0, g)),
        out_shape=jax.ShapeDtypeStruct((B, S, A_HEADS * HEAD_DIM), BF16),
        scratch_shapes=[
            pltpu.VMEM((S, LANES), F32),
            pltpu.VMEM((MAX_WINDOW + S, LANES), F32),
            pltpu.VMEM((MAX_WINDOW + S, LANES), F32),
            pltpu.VMEM((_spread_row(S), LANES), F32),
            pltpu.VMEM((_spread_row(MAX_WINDOW + S), LANES), F32),
            pltpu.VMEM((_spread_row(MAX_WINDOW + S), LANES), F32),
            pltpu.VMEM((3, S, LANES), F32),
            pltpu.VMEM((3, S, LANES), F32),
            pltpu.VMEM((2 * len(DILATED_CONFIGS), 2 * BAND, 2 * BAND), F32),
        ],
        compiler_params=_params(("parallel", "parallel")),
        name="dilated_attn",
    )(slopes, proj3, proj3, proj3)


TQ = 512
KV_UNROLL = 2
DIFF_HEADS = 4


def _diff_kernel(slopes_ref, q_ref, k_ref, v_ref, lam_ref, sg_ref, o_ref,
                 m_ref, acc_ref, *, lam_init):
    hg = pl.program_id(1)
    qi = pl.program_id(2)
    lo = lax.broadcasted_iota(jnp.int32, (TQ, LANES), 1) < HEAD_DIM
    kidx = lax.broadcasted_iota(jnp.int32, (1, TQ), 1).astype(F32)
    ones = jnp.ones((TQ, LANES), BF16)
    heads = range(DIFF_HEADS)
    cols = [slice(hh * LANES, (hh + 1) * LANES) for hh in heads]
    slopes = [slopes_ref[hg * DIFF_HEADS + hh] for hh in heads]
    q2 = []
    for hh in heads:
        q = q_ref[:, cols[hh]]
        zero = jnp.zeros_like(q)
        q2.append(jnp.concatenate([jnp.where(lo, q, zero), jnp.where(lo, zero, q)], axis=0))

    m_ref[...] = jnp.full(m_ref.shape, NEG_INF, F32)
    acc_ref[...] = jnp.zeros(acc_ref.shape, F32)

    def step(j, masked, hh):
        ks = pl.ds(pl.multiple_of(j * TQ, TQ), TQ)
        k = k_ref[ks, cols[hh]]
        v1 = jnp.concatenate([v_ref[ks, cols[hh]], ones], axis=1)
        s = _dot_nt(q2[hh], k) + slopes[hh] * (kidx + (j * TQ).astype(F32))
        if masked:
            rel = (lax.broadcasted_iota(jnp.int32, (2 * TQ, TQ), 0) % TQ
                   - lax.broadcasted_iota(jnp.int32, (2 * TQ, TQ), 1))
            s = jnp.where(rel >= 0, s, NEG_INF)
        m_old = m_ref[hh]
        m_new = jnp.maximum(m_old, jnp.max(s, axis=1, keepdims=True))
        alpha = jnp.exp2(m_old - m_new)
        e = jnp.exp2(s - jnp.concatenate([m_new] * (TQ // LANES), axis=1))
        acc_ref[hh] = (jnp.concatenate([alpha, alpha], axis=1) * acc_ref[hh]
                       + _dot(e.astype(BF16), v1))
        m_ref[hh] = m_new

    def body(jq, carry):
        for u in range(KV_UNROLL):
            for hh in heads:
                step(KV_UNROLL * jq + u, False, hh)
        return carry
    nq = qi // KV_UNROLL
    lax.fori_loop(0, nq, body, 0)

    rem = qi - nq * KV_UNROLL
    for r in range(KV_UNROLL):
        @pl.when(rem == r)
        def _(r=r):
            for u in range(r):
                for hh in heads:
                    step(qi - r + u, False, hh)
            for hh in heads:
                step(qi, True, hh)

    lq = lam_ref[...]
    lam = (jnp.exp(jnp.sum(lq[0:1] * lq[1:2], axis=1, keepdims=True))
           - jnp.exp(jnp.sum(lq[2:3] * lq[3:4], axis=1, keepdims=True)) + lam_init)
    for hh in heads:
        on = acc_ref[hh, :, 0:LANES] / acc_ref[hh, :, LANES:]
        o = on[:TQ] - lam * on[TQ:]
        o = _rms_rows(o, sg_ref[...]) * (1.0 - lam_init)
        o_ref[:, cols[hh]] = o.astype(o_ref.dtype)


def _diff_attention(proj3, slopes, lam_vecs, sub_gain, lam_init):
    B, S, _ = proj3.shape
    width = DIFF_HEADS * LANES
    groups = B_HEADS // DIFF_HEADS
    qoff = 3 * A_HEADS * HEAD_DIM // width
    koff = qoff + groups
    voff = koff + groups
    return pl.pallas_call(
        functools.partial(_diff_kernel, lam_init=lam_init),
        grid=(B, groups, S // TQ),
        in_specs=[
            pl.BlockSpec(memory_space=pltpu.SMEM),
            pl.BlockSpec((None, TQ, width), lambda b, h, i: (b, i, qoff + h)),
            pl.BlockSpec((None, S, width), lambda b, h, i: (b, 0, koff + h)),
            pl.BlockSpec((None, S, width), lambda b, h, i: (b, 0, voff + h)),
            pl.BlockSpec((4, HEAD_DIM), lambda b, h, i: (0, 0)),
            pl.BlockSpec((1, 2 * HEAD_DIM), lambda b, h, i: (0, 0)),
        ],
        out_specs=pl.BlockSpec((None, TQ, width), lambda b, h, i: (b, i, h)),
        out_shape=jax.ShapeDtypeStruct((B, S, B_HEADS * 2 * HEAD_DIM), BF16),
        scratch_shapes=[
            pltpu.VMEM((DIFF_HEADS, 2 * TQ, LANES), F32),
            pltpu.VMEM((DIFF_HEADS, 2 * TQ, 2 * LANES), F32),
        ],
        compiler_params=_params(("parallel", "parallel", "parallel")),
        name="diff_attn",
    )(slopes, proj3, proj3, proj3, lam_vecs, sub_gain)


FFN_CHUNK = 512


def _seg_pitch(seg):
    p = seg // SUBLANES + 1
    return SUBLANES * (p if p % 2 else p + 1)


def _to_segment_rows(src, stage, dst_ref):
    tm, C = src.shape
    seg = tm // SUBLANES
    pitch = _seg_pitch(seg)
    for n in range(C // LANES):
        cols = slice(n * LANES, (n + 1) * LANES)
        for s in range(SUBLANES):
            stage[n, s * pitch:s * pitch + seg, :] = src[s * seg:(s + 1) * seg, cols]
    for j in range(seg):
        dst_ref[j * SUBLANES:(j + 1) * SUBLANES, :] = jnp.concatenate(
            [stage[n, pl.ds(j, SUBLANES, stride=pitch), :] for n in range(C // LANES)], axis=1)


def _from_segment_rows(val, stage, dst_ref):
    tm, C = val.shape
    seg = tm // SUBLANES
    pitch = _seg_pitch(seg)
    for n in range(C // LANES):
        cols = slice(n * LANES, (n + 1) * LANES)
        for j in range(seg):
            stage[n, pl.ds(j, SUBLANES, stride=pitch), :] = val[j * SUBLANES:(j + 1) * SUBLANES, cols]
        for s in range(SUBLANES):
            dst_ref[s * seg:(s + 1) * seg, cols] = stage[n, s * pitch:s * pitch + seg, :]


def _segment_rows_ref(src, seg_in, stage, hseg):
    if seg_in:
        return src
    _to_segment_rows(src, stage, hseg)
    return hseg


def _store_segment_rows(val, seg_out, stage, o_ref):
    if seg_out:
        o_ref[...] = val
    else:
        _from_segment_rows(val, stage, o_ref)


def _segment_conv(u, carry_ref, cols, w_ref, b_ref, taps):
    tm = u.shape[0]
    first_sublane = lax.broadcasted_iota(jnp.int32, (SUBLANES, u.shape[1]), 0) == 0
    wrapped = []
    for i in range(1, taps):
        tail = u[tm - i * SUBLANES:tm - (i - 1) * SUBLANES]
        wrapped.append(jnp.where(first_sublane,
                                 pltpu.roll(carry_ref[i - 1, :, cols], 1, 0),
                                 pltpu.roll(tail, 1, 0)))
        carry_ref[i - 1, :, cols] = tail
    out = b_ref[:, cols] + w_ref[taps - 1:taps, cols] * u
    for k in range(1, taps):
        shifted = jnp.concatenate(wrapped[k - 1::-1] + [u[:tm - k * SUBLANES]], axis=0)
        out = out + w_ref[taps - 1 - k:taps - k, cols] * shifted
    return out


def _ffn_kernel(*refs, taps, with_attn, seg_in, seg_out):
    if with_attn:
        x_ref, a_ref, b_ref, wo_ref, *refs = refs
        na = a_ref.shape[1]
        h = x_ref[...] + _dot(a_ref[...], wo_ref[0:na, :]) + _dot(b_ref[...], wo_ref[na:, :])
    else:
        h, *refs = refs
    g_ref, wup_ref, cw_ref, cb_ref, wdn_ref, o_ref, stage, hseg, carry, act = refs
    F = wdn_ref.shape[0]

    @pl.when(pl.program_id(1) == 0)
    def _():
        carry[...] = jnp.zeros(carry.shape, F32)

    hs_ref = _segment_rows_ref(h, seg_in, stage, hseg)
    xn = _rms_rows(hs_ref[...], g_ref[...]).astype(BF16)
    for c in range(F // FFN_CHUNK):
        gc = slice(c * FFN_CHUNK, (c + 1) * FFN_CHUNK)
        vc = slice(F + c * FFN_CHUNK, F + (c + 1) * FFN_CHUNK)
        gg = _segment_conv(_dot(xn, wup_ref[:, gc]), carry, gc, cw_ref, cb_ref, taps)
        vv = _segment_conv(_dot(xn, wup_ref[:, vc]), carry, vc, cw_ref, cb_ref, taps)
        act[:, gc] = (_gelu(gg) * vv).astype(BF16)
    _store_segment_rows(hs_ref[...] + _dot(act[...], wdn_ref[...]), seg_out, stage, o_ref)


def _stage_shape(tm, C):
    return (C // LANES, SUBLANES * _seg_pitch(tm // SUBLANES), LANES)


def _resident(shape):
    return pl.BlockSpec(shape, lambda b, i: (0,) * len(shape), pipeline_mode=pl.Buffered(1))


SEQ_TILE = 512


def _conv_ffn(h3, g, wup, cw, cb, wdn, attn=None, seg_in=False, seg_out=False, tm=SEQ_TILE):
    assert not (seg_in and attn is not None)
    B, S, D = h3.shape
    F2 = wup.shape[1]
    taps = cw.shape[0]
    row_tile = lambda width: pl.BlockSpec((None, tm, width), lambda b, i: (b, i, 0))
    attn_args, attn_specs = (), []
    if attn is not None:
        oa, ob, wo = attn
        attn_args = (oa, ob, wo)
        attn_specs = [row_tile(oa.shape[2]), row_tile(ob.shape[2]), _resident(wo.shape)]
    return pl.pallas_call(
        functools.partial(_ffn_kernel, taps=taps, with_attn=attn is not None,
                          seg_in=seg_in, seg_out=seg_out),
        grid=(B, S // tm),
        in_specs=[
            row_tile(D),
            *attn_specs,
            _resident((1, D)),
            _resident(wup.shape),
            _resident(cw.shape),
            _resident((1, F2)),
            _resident(wdn.shape),
        ],
        out_specs=row_tile(D),
        out_shape=jax.ShapeDtypeStruct((B, S, D), F32),
        scratch_shapes=[
            pltpu.VMEM(_stage_shape(tm, D), F32),
            pltpu.VMEM((tm, D), F32),
            pltpu.VMEM((taps - 1, SUBLANES, F2), F32),
            pltpu.VMEM((tm, F2 // 2), BF16),
        ],
        compiler_params=_params(("arbitrary", "arbitrary")),
        name="conv_ffn",
    )(h3, *attn_args, g, wup, cw, cb, wdn)


def _rec_kernel(h_ref, g_ref, win_ref, cw_ref, cb_ref, wa_ref, ba_ref, wx_ref, bx_ref,
                ap_ref, wout_ref, o_ref, stage, hseg, carry, hstate, *, taps, seg_in, seg_out):
    tm, C = h_ref.shape
    seg = tm // SUBLANES
    bw = C // LRU_BLOCKS

    @pl.when(pl.program_id(1) == 0)
    def _():
        carry[...] = jnp.zeros(carry.shape, F32)
        hstate[...] = jnp.zeros(hstate.shape, F32)

    hs_ref = _segment_rows_ref(h_ref, seg_in, stage, hseg)
    xn = _rms_rows(hs_ref[...], g_ref[...]).astype(BF16)
    gate = _dot(xn, win_ref[:, 0:C])
    xr_all = _segment_conv(_dot(xn, win_ref[:, C:]), carry, slice(0, C), cw_ref, cb_ref, taps)

    ap = ap_ref[...]
    decay = -LRU_C * (jnp.maximum(-ap, 0.0) + jnp.log1p(jnp.exp(-jnp.abs(ap))))
    sublane = lax.broadcasted_iota(jnp.int32, (SUBLANES, bw), 0)
    blocks = []
    for n in range(LRU_BLOCKS):
        cols = slice(n * bw, (n + 1) * bw)
        xr = xr_all[:, cols]
        xb = xr.astype(BF16)
        r = _sigmoid(_dot(xb, wa_ref[n]) + ba_ref[:, cols])
        i = _sigmoid(_dot(xb, wx_ref[n]) + bx_ref[:, cols])
        log_a = decay[:, cols] * r
        a = jnp.exp(log_a)
        u = jnp.sqrt(-jnp.tanh(log_a) * (1.0 + a * a)) * (i * xr)

        hl = jnp.zeros((SUBLANES, bw), F32)
        pp = jnp.ones((SUBLANES, bw), F32)
        hls, pps = [], []
        for j in range(seg):
            aj = a[j * SUBLANES:(j + 1) * SUBLANES]
            hl = aj * hl + u[j * SUBLANES:(j + 1) * SUBLANES]
            pp = aj * pp
            hls.append(hl)
            pps.append(pp)
        cin = hstate[0:1, cols]
        h_in = jnp.zeros((SUBLANES, bw), F32)
        for s in range(SUBLANES):
            h_in = jnp.where(sublane == s, cin, h_in)
            cin = hl[s:s + 1] + pp[s:s + 1] * cin
        hstate[:, cols] = jnp.broadcast_to(cin, (SUBLANES, bw))
        blocks.append(jnp.concatenate([hls[j] + pps[j] * h_in for j in range(seg)], axis=0))

    hs = jnp.concatenate(blocks, axis=1)
    y = (hs * _gelu(gate)).astype(BF16)
    _store_segment_rows(hs_ref[...] + _dot(y, wout_ref[...]), seg_out, stage, o_ref)


def _recurrent_block(h3, g, win, cw, cb, wa, ba, wx, bx, ap, wout,
                     seg_in=False, seg_out=False, tm=SEQ_TILE):
    B, S, D = h3.shape
    C = wout.shape[0]
    taps = cw.shape[0]
    unused = (SUBLANES, LANES)
    stage_shape = unused if (seg_in and seg_out) else _stage_shape(tm, D)
    hseg_shape = unused if seg_in else (tm, D)
    return pl.pallas_call(
        functools.partial(_rec_kernel, taps=taps, seg_in=seg_in, seg_out=seg_out),
        grid=(B, S // tm),
        in_specs=[
            pl.BlockSpec((None, tm, D), lambda b, i: (b, i, 0)),
            _resident((1, D)), _resident(win.shape), _resident(cw.shape), _resident((1, C)),
            _resident(wa.shape), _resident((1, C)), _resident(wx.shape), _resident((1, C)),
            _resident((1, C)), _resident(wout.shape),
        ],
        out_specs=pl.BlockSpec((None, tm, D), lambda b, i: (b, i, 0)),
        out_shape=jax.ShapeDtypeStruct((B, S, D), F32),
        scratch_shapes=[
            pltpu.VMEM(stage_shape, F32),
            pltpu.VMEM(hseg_shape, F32),
            pltpu.VMEM((taps - 1, SUBLANES, C), F32),
            pltpu.VMEM((SUBLANES, C), F32),
        ],
        compiler_params=_params(("arbitrary", "arbitrary")),
        name="recurrent_block",
    )(h3, g, win, cw, cb, wa, ba, wx, bx, ap, wout)


def _alibi_slopes(n):
    return jnp.exp2(-8.0 * jnp.arange(1, n + 1, dtype=F32) / n)


def _row(v):
    return v.reshape(1, -1).astype(F32)


def kernel(x, attn_norm, attn_w_in, attn_w_out, a_q_norm, a_k_norm, b_q_norm, b_k_norm, b_sub_norm,
           b_lam_q1, b_lam_k1, b_lam_q2, b_lam_k2, rec_norm, rec_w_in, rec_conv_w, rec_conv_b,
           rec_gate_a_w, rec_gate_a_b, rec_gate_x_w, rec_gate_x_b, rec_a_param, rec_w_out,
           ffn_norm, ffn_w_up, ffn_conv_w, ffn_conv_b, ffn_w_down):
    B, S, D = x.shape
    depth = ffn_norm.shape[0]
    slopes = _alibi_slopes(A_HEADS + B_HEADS)
    h = x
    h_is_seg = False
    for layer in range(depth):
        j = layer // 2
        attn = None
        if layer % 2 == 0:
            lam_init = 0.8 - 0.6 * math.exp(-0.3 * layer)
            scale = HEAD_DIM ** -0.5
            reps = SEC // HEAD_DIM
            ones = jnp.ones((SEC,), F32)
            head_gains = jnp.stack([
                jnp.tile(a_q_norm[j].astype(F32), reps) * (scale * LOG2E),
                jnp.tile(a_k_norm[j].astype(F32), reps), ones,
                jnp.tile(b_q_norm[j].astype(F32), reps) * (scale * LOG2E),
                jnp.tile(b_k_norm[j].astype(F32), reps), ones])
            proj = _attn_inproj(h.reshape(B * S, D), _row(attn_norm[j]),
                                attn_w_in[j].astype(BF16), head_gains)
            proj3 = proj.reshape(B, S, -1)
            oa = _dilated_attention(proj3, slopes[:A_HEADS] * LOG2E)
            lam_vecs = jnp.stack([b_lam_q1[j], b_lam_k1[j], b_lam_q2[j], b_lam_k2[j]]).astype(F32)
            ob = _diff_attention(proj3, slopes[A_HEADS:] * LOG2E, lam_vecs, _row(b_sub_norm[j]),
                                 lam_init)
            attn = (oa, ob, attn_w_out[j].astype(BF16))
        else:
            h = _recurrent_block(
                h, _row(rec_norm[j]), rec_w_in[j].astype(BF16), rec_conv_w[j].astype(F32),
                _row(rec_conv_b[j]), rec_gate_a_w[j].astype(BF16), _row(rec_gate_a_b[j]),
                rec_gate_x_w[j].astype(BF16), _row(rec_gate_x_b[j]), _row(rec_a_param[j]),
                rec_w_out[j].astype(BF16), seg_in=h_is_seg, seg_out=True)
            h_is_seg = True
        next_is_rec = layer + 1 < depth and (layer + 1) % 2 == 1
        h = _conv_ffn(h, _row(ffn_norm[layer]), ffn_w_up[layer].astype(BF16),
                      ffn_conv_w[layer].astype(F32), _row(ffn_conv_b[layer]),
                      ffn_w_down[layer].astype(BF16), attn=attn,
                      seg_in=h_is_seg, seg_out=next_is_rec)
        h_is_seg = next_is_rec
    return h
```

```python
import functools
import math

import numpy as np
import jax
import jax.numpy as jnp
from jax import lax
from jax.experimental import pallas as pl
from jax.experimental.pallas import tpu as pltpu

F32 = jnp.float32
BF16 = jnp.bfloat16

HEAD_DIM = 64
A_HEADS = 8
B_HEADS = 4
DILATED_CONFIGS = ((128, 1), (512, 4), (2048, 16))
BAND = 128
MAX_WINDOW = 2048
LRU_BLOCKS = 8
LRU_C = 8.0
NORM_EPS = 1e-6
NEG_INF = -1e30
LOG2E = math.log2(math.e)
LANES = 128
SUBLANES = 8
VMEM_LIMIT = 56 * 1024 * 1024


def _gelu(x):
    c = math.sqrt(2.0 / math.pi)
    return x * (0.5 * (1.0 + jnp.tanh(c * (x + 0.044715 * (x * x * x)))))


def _sigmoid(x):
    return 1.0 / (1.0 + jnp.exp(-x))


def _rms_rows(x, g):
    ms = jnp.mean(x * x, axis=-1, keepdims=True)
    return x * lax.rsqrt(ms + NORM_EPS) * g


def _dot(a, b):
    return jnp.dot(a, b, preferred_element_type=F32)


def _dot_nt(a, b):
    return lax.dot_general(a, b, (((1,), (1,)), ((), ())), preferred_element_type=F32)


def _params(sem):
    return pltpu.CompilerParams(dimension_semantics=sem, vmem_limit_bytes=VMEM_LIMIT)


SEC = 512
N_SEC = 6
MXU_TILE = 256


def _inproj_kernel(x_ref, g_ref, w_ref, hg_ref, p_ref, o_ref):
    xn = _rms_rows(x_ref[...], g_ref[...]).astype(BF16)
    for s in range(N_SEC):
        y = _dot(xn, w_ref[:, s * SEC:(s + 1) * SEC])
        if s % 3 == 2:
            o_ref[:, s * SEC:(s + 1) * SEC] = y.astype(BF16)
        else:
            y2 = (y * y).astype(BF16)
            ms = jnp.concatenate(
                [_dot(y2[:, c:c + MXU_TILE], p_ref[...]) for c in range(0, SEC, MXU_TILE)], axis=1)
            o_ref[:, s * SEC:(s + 1) * SEC] = (
                y * lax.rsqrt(ms + NORM_EPS) * hg_ref[s:s + 1, :]).astype(BF16)


def _attn_inproj(x2, g, w, head_gains, tm=512):
    T, D = x2.shape
    N = w.shape[1]
    blk = np.kron(np.eye(MXU_TILE // HEAD_DIM), np.full((HEAD_DIM, HEAD_DIM), 1.0 / HEAD_DIM))
    pmat = jnp.asarray(blk, dtype=BF16)
    return pl.pallas_call(
        _inproj_kernel,
        grid=(T // tm,),
        in_specs=[
            pl.BlockSpec((tm, D), lambda i: (i, 0)),
            pl.BlockSpec((1, D), lambda i: (0, 0)),
            pl.BlockSpec((D, N), lambda i: (0, 0)),
            pl.BlockSpec((N_SEC, SEC), lambda i: (0, 0)),
            pl.BlockSpec((MXU_TILE, MXU_TILE), lambda i: (0, 0)),
        ],
        out_specs=pl.BlockSpec((tm, N), lambda i: (i, 0)),
        out_shape=jax.ShapeDtypeStruct((T, N), BF16),
        compiler_params=_params(("parallel",)),
        name="attn_inproj",
    )(x2, g, w, head_gains, pmat)


ROW_GROUP = 16
ROW_PITCH = 24


def _spread_row(t):
    return (t // ROW_GROUP) * ROW_PITCH + t % ROW_GROUP


def _dilated_kernel(slopes_ref, q_ref, k_ref, v_ref, o_ref, qf, kf, vf, qp, kp, vp, ob, db, mb,
                    bias_buf):
    S = q_ref.shape[0]
    pad = kf.shape[0] - S
    g = pl.program_id(1)
    qf[...] = q_ref[...].astype(F32)
    kf[0:pad, :] = jnp.zeros((pad, LANES), F32)
    vf[0:pad, :] = jnp.zeros((pad, LANES), F32)
    kf[pad:, :] = k_ref[...].astype(F32)
    vf[pad:, :] = v_ref[...].astype(F32)
    kp[0:_spread_row(pad), :] = jnp.zeros((_spread_row(pad), LANES), F32)
    vp[0:_spread_row(pad), :] = jnp.zeros((_spread_row(pad), LANES), F32)

    def spread(grp, carry):
        src = pl.ds(pl.multiple_of(grp * ROW_GROUP, ROW_GROUP), ROW_GROUP)
        dst = pl.ds(pl.multiple_of(grp * ROW_PITCH, SUBLANES), ROW_GROUP)
        dst_kv = pl.ds(pl.multiple_of(grp * ROW_PITCH + _spread_row(pad), SUBLANES), ROW_GROUP)
        qp[dst, :] = q_ref[src, :].astype(F32)
        kp[dst_kv, :] = k_ref[src, :].astype(F32)
        vp[dst_kv, :] = v_ref[src, :].astype(F32)
        return carry
    lax.fori_loop(0, S // ROW_GROUP, spread, 0, unroll=8)

    lo = lax.broadcasted_iota(jnp.int32, (BAND, LANES), 1) < HEAD_DIM
    ii = lax.broadcasted_iota(jnp.int32, (BAND, 2 * BAND), 0)
    jj = lax.broadcasted_iota(jnp.int32, (BAND, 2 * BAND), 1)
    delta = ii + BAND - jj
    in_band = (delta >= 0) & (delta <= BAND)
    prev_half = lax.broadcasted_iota(jnp.int32, (2 * BAND, 2 * BAND), 1) < BAND
    sl0 = slopes_ref[2 * g]
    sl1 = slopes_ref[2 * g + 1]
    ones = jnp.ones((2 * BAND, LANES), BF16)

    for bi, (window, d) in enumerate(DILATED_CONFIGS):
        span = BAND * d
        nbs = S // span
        dist = (delta * d).astype(F32)
        bias = jnp.concatenate(
            [jnp.where(in_band, -sl0 * dist, NEG_INF),
             jnp.where(in_band, -sl1 * dist, NEG_INF)], axis=0)
        bias_buf[2 * bi] = bias
        bias_buf[2 * bi + 1] = jnp.where(prev_half, NEG_INF, bias)

        def body(blk, carry, d=d, span=span, nbs=nbs, bi=bi):
            r = blk // nbs
            nb = blk - r * nbs
            start = nb * span + r
            if d % ROW_GROUP == 0:
                stride = d // ROW_GROUP * ROW_PITCH
                q_row = nb * _spread_row(span) + r
                kv_row = q_row + _spread_row(pad) - _spread_row(span)
                q = qp[pl.ds(q_row, BAND, stride=stride), :].astype(BF16)
                k = kp[pl.ds(kv_row, 2 * BAND, stride=stride), :].astype(BF16)
                v = vp[pl.ds(kv_row, 2 * BAND, stride=stride), :].astype(BF16)
            else:
                q = qf[pl.ds(start, BAND, stride=d), :].astype(BF16)
                k = kf[pl.ds(pad + start - span, 2 * BAND, stride=d), :].astype(BF16)
                v = vf[pl.ds(pad + start - span, 2 * BAND, stride=d), :].astype(BF16)
            zero = jnp.zeros_like(q)
            q2 = jnp.concatenate([jnp.where(lo, q, zero), jnp.where(lo, zero, q)], axis=0)
            s = _dot_nt(q2, k) + bias_buf[2 * bi + jnp.where(nb == 0, 1, 0)]
            m = jnp.max(s, axis=1, keepdims=True)
            e = jnp.exp2(s - m)
            pv = _dot(e.astype(BF16), jnp.concatenate([v, ones], axis=1))
            rows = pl.ds(start, BAND, stride=d)
            ob[bi, rows, :] = jnp.where(lo, pv[:BAND, :LANES], pv[BAND:, :LANES])
            db[bi, rows, :] = jnp.where(lo, pv[:BAND, LANES:], pv[BAND:, LANES:])
            mb[bi, rows, :] = jnp.where(lo, jnp.broadcast_to(m[:BAND], (BAND, LANES)),
                                        jnp.broadcast_to(m[BAND:], (BAND, LANES)))
            return carry

        lax.fori_loop(0, S // BAND, body, 0, unroll=16)

    rows = 512
    def mix(c, carry):
        sl = pl.ds(pl.multiple_of(c * rows, rows), rows)
        m0, m1, m2 = mb[0, sl, :], mb[1, sl, :], mb[2, sl, :]
        m = jnp.maximum(jnp.maximum(m0, m1), m2)
        w0, w1, w2 = jnp.exp2(m0 - m), jnp.exp2(m1 - m), jnp.exp2(m2 - m)
        num = w0 * ob[0, sl, :] + w1 * ob[1, sl, :] + w2 * ob[2, sl, :]
        den = w0 * db[0, sl, :] + w1 * db[1, sl, :] + w2 * db[2, sl, :]
        o_ref[sl, :] = (num / den).astype(o_ref.dtype)
        return carry
    lax.fori_loop(0, S // rows, mix, 0)


def _dilated_attention(proj3, slopes):
    B, S, _ = proj3.shape
    assert S % MAX_WINDOW == 0
    pairs = A_HEADS // 2
    blk = lambda off: pl.BlockSpec((None, S, LANES), lambda b, g, off=off: (b, 0, off + g))
    return pl.pallas_call(
        _dilated_kernel,
        grid=(B, pairs),
        in_specs=[
            pl.BlockSpec(memory_space=pltpu.SMEM),
            blk(0), blk(pairs), blk(2 * pairs),
        ],
        out_specs=pl.BlockSpec((None, S, LANES), lambda b, g: (b, 0, g)),
        out_shape=jax.ShapeDtypeStruct((B, S, A_HEADS * HEAD_DIM), BF16),
        scratch_shapes=[
            pltpu.VMEM((S, LANES), F32),
            pltpu.VMEM((MAX_WINDOW + S, LANES), F32),
            pltpu.VMEM((MAX_WINDOW + S, LANES), F32),
            pltpu.VMEM((_spread_row(S), LANES), F32),
            pltpu.VMEM((_spread_row(MAX_WINDOW + S), LANES), F32),
            pltpu.VMEM((_spread_row(MAX_WINDOW + S), LANES), F32),
            pltpu.VMEM((len(DILATED_CONFIGS), S, LANES), F32),
            pltpu.VMEM((len(DILATED_CONFIGS), S, LANES), F32),
            pltpu.VMEM((len(DILATED_CONFIGS), S, LANES), F32),
            pltpu.VMEM((2 * len(DILATED_CONFIGS), 2 * BAND, 2 * BAND), F32),
        ],
        compiler_params=_params(("parallel", "parallel")),
        name="dilated_attn",
    )(slopes, proj3, proj3, proj3)


TQ = 512
KV_UNROLL = 2
DIFF_HEADS = 4


def _diff_kernel(slopes_ref, q_ref, k_ref, v_ref, lam_ref, sg_ref, o_ref,
                 m_ref, acc_ref, *, lam_init):
    hg = pl.program_id(1)
    qi = pl.program_id(2)
    lo = lax.broadcasted_iota(jnp.int32, (TQ, LANES), 1) < HEAD_DIM
    heads = range(DIFF_HEADS)
    cols = [slice(hh * LANES, (hh + 1) * LANES) for hh in heads]
    slopes = [slopes_ref[hg * DIFF_HEADS + hh] for hh in heads]
    q2 = []
    for hh in heads:
        q = q_ref[:, cols[hh]]
        zero = jnp.zeros_like(q)
        q2.append(jnp.concatenate([jnp.where(lo, q, zero), jnp.where(lo, zero, q)], axis=0))

    m_ref[...] = jnp.full(m_ref.shape, NEG_INF, F32)
    acc_ref[...] = jnp.zeros(acc_ref.shape, F32)

    def update(hh, key0, nk, row_blocks, masked):
        ks = pl.ds(pl.multiple_of(key0, nk), nk)
        k = k_ref[ks, cols[hh]]
        v1 = jnp.concatenate([v_ref[ks, cols[hh]], jnp.ones((nk, LANES), BF16)], axis=1)
        qs = jnp.concatenate([q2[hh][r0:r0 + rn] for r0, rn in row_blocks], axis=0)
        nrows = qs.shape[0]
        kpos = lax.broadcasted_iota(jnp.int32, (1, nk), 1).astype(F32) + key0.astype(F32)
        s = _dot_nt(qs, k) + slopes[hh] * kpos
        if masked:
            rn = row_blocks[0][1]
            rel = (lax.broadcasted_iota(jnp.int32, (nrows, nk), 0) % rn
                   - lax.broadcasted_iota(jnp.int32, (nrows, nk), 1))
            s = jnp.where(rel >= 0, s, NEG_INF)
        m_old = jnp.concatenate([m_ref[hh, r0:r0 + rn] for r0, rn in row_blocks], axis=0)
        a_old = jnp.concatenate([acc_ref[hh, r0:r0 + rn] for r0, rn in row_blocks], axis=0)
        m_new = jnp.maximum(m_old, jnp.max(s, axis=1, keepdims=True))
        alpha = jnp.exp2(m_old - m_new)
        e = jnp.exp2(s - jnp.concatenate([m_new] * (nk // LANES), axis=1))
        a_new = jnp.concatenate([alpha, alpha], axis=1) * a_old + _dot(e.astype(BF16), v1)
        at = 0
        for r0, rn in row_blocks:
            m_ref[hh, r0:r0 + rn] = m_new[at:at + rn]
            acc_ref[hh, r0:r0 + rn] = a_new[at:at + rn]
            at += rn

    half = TQ // 2

    def step(j, masked, hh):
        if not masked:
            update(hh, j * TQ, TQ, [(0, 2 * TQ)], False)
            return
        update(hh, j * TQ, half, [(0, TQ), (TQ, TQ)], True)
        update(hh, j * TQ + half, half, [(half, half), (TQ + half, half)], True)

    def body(jq, carry):
        for u in range(KV_UNROLL):
            for hh in heads:
                step(KV_UNROLL * jq + u, False, hh)
        return carry
    nq = qi // KV_UNROLL
    lax.fori_loop(0, nq, body, 0)

    rem = qi - nq * KV_UNROLL
    for r in range(KV_UNROLL):
        @pl.when(rem == r)
        def _(r=r):
            for u in range(r):
                for hh in heads:
                    step(qi - r + u, False, hh)
            for hh in heads:
                step(qi, True, hh)

    lq = lam_ref[...]
    lam = (jnp.exp(jnp.sum(lq[0:1] * lq[1:2], axis=1, keepdims=True))
           - jnp.exp(jnp.sum(lq[2:3] * lq[3:4], axis=1, keepdims=True)) + lam_init)
    for hh in heads:
        on = acc_ref[hh, :, 0:LANES] / acc_ref[hh, :, LANES:]
        o = on[:TQ] - lam * on[TQ:]
        o = _rms_rows(o, sg_ref[...]) * (1.0 - lam_init)
        o_ref[:, cols[hh]] = o.astype(o_ref.dtype)


def _diff_attention(proj3, slopes, lam_vecs, sub_gain, lam_init):
    B, S, _ = proj3.shape
    width = DIFF_HEADS * LANES
    groups = B_HEADS // DIFF_HEADS
    qoff = 3 * A_HEADS * HEAD_DIM // width
    koff = qoff + groups
    voff = koff + groups
    return pl.pallas_call(
        functools.partial(_diff_kernel, lam_init=lam_init),
        grid=(B, groups, S // TQ),
        in_specs=[
            pl.BlockSpec(memory_space=pltpu.SMEM),
            pl.BlockSpec((None, TQ, width), lambda b, h, i: (b, i, qoff + h)),
            pl.BlockSpec((None, S, width), lambda b, h, i: (b, 0, koff + h)),
            pl.BlockSpec((None, S, width), lambda b, h, i: (b, 0, voff + h)),
            pl.BlockSpec((4, HEAD_DIM), lambda b, h, i: (0, 0)),
            pl.BlockSpec((1, 2 * HEAD_DIM), lambda b, h, i: (0, 0)),
        ],
        out_specs=pl.BlockSpec((None, TQ, width), lambda b, h, i: (b, i, h)),
        out_shape=jax.ShapeDtypeStruct((B, S, B_HEADS * 2 * HEAD_DIM), BF16),
        scratch_shapes=[
            pltpu.VMEM((DIFF_HEADS, 2 * TQ, LANES), F32),
            pltpu.VMEM((DIFF_HEADS, 2 * TQ, 2 * LANES), F32),
        ],
        compiler_params=_params(("parallel", "parallel", "parallel")),
        name="diff_attn",
    )(slopes, proj3, proj3, proj3, lam_vecs, sub_gain)


FFN_CHUNK = 512


def _seg_pitch(seg):
    p = seg // SUBLANES + 1
    return SUBLANES * (p if p % 2 else p + 1)


def _to_segment_rows(src, stage, dst_ref):
    tm, C = src.shape
    seg = tm // SUBLANES
    pitch = _seg_pitch(seg)
    for n in range(C // LANES):
        cols = slice(n * LANES, (n + 1) * LANES)
        for s in range(SUBLANES):
            stage[n, s * pitch:s * pitch + seg, :] = src[s * seg:(s + 1) * seg, cols]
    for j in range(seg):
        dst_ref[j * SUBLANES:(j + 1) * SUBLANES, :] = jnp.concatenate(
            [stage[n, pl.ds(j, SUBLANES, stride=pitch), :] for n in range(C // LANES)], axis=1)


def _from_segment_rows(val, stage, dst_ref):
    tm, C = val.shape
    seg = tm // SUBLANES
    pitch = _seg_pitch(seg)
    for n in range(C // LANES):
        cols = slice(n * LANES, (n + 1) * LANES)
        for j in range(seg):
            stage[n, pl.ds(j, SUBLANES, stride=pitch), :] = val[j * SUBLANES:(j + 1) * SUBLANES, cols]
        for s in range(SUBLANES):
            dst_ref[s * seg:(s + 1) * seg, cols] = stage[n, s * pitch:s * pitch + seg, :]


def _segment_rows_ref(src, seg_in, stage, hseg):
    if seg_in:
        return src
    _to_segment_rows(src, stage, hseg)
    return hseg


def _store_segment_rows(val, seg_out, stage, o_ref):
    if seg_out:
        o_ref[...] = val
    else:
        _from_segment_rows(val, stage, o_ref)


def _segment_conv(u, carry_ref, cols, w_ref, b_ref, taps):
    tm = u.shape[0]
    first_sublane = lax.broadcasted_iota(jnp.int32, (SUBLANES, u.shape[1]), 0) == 0
    wrapped = []
    for i in range(1, taps):
        tail = u[tm - i * SUBLANES:tm - (i - 1) * SUBLANES]
        wrapped.append(jnp.where(first_sublane,
                                 pltpu.roll(carry_ref[i - 1, :, cols], 1, 0),
                                 pltpu.roll(tail, 1, 0)))
        carry_ref[i - 1, :, cols] = tail
    out = b_ref[:, cols] + w_ref[taps - 1:taps, cols] * u
    for k in range(1, taps):
        shifted = jnp.concatenate(wrapped[k - 1::-1] + [u[:tm - k * SUBLANES]], axis=0)
        out = out + w_ref[taps - 1 - k:taps - k, cols] * shifted
    return out


def _ffn_kernel(*refs, taps, with_attn, seg_in, seg_out):
    if with_attn:
        x_ref, a_ref, b_ref, wo_ref, *refs = refs
        na = a_ref.shape[1]
        h = x_ref[...] + _dot(a_ref[...], wo_ref[0:na, :]) + _dot(b_ref[...], wo_ref[na:, :])
    else:
        h, *refs = refs
    g_ref, wup_ref, cw_ref, cb_ref, wdn_ref, o_ref, stage, hseg, carry, act = refs
    F = wdn_ref.shape[0]

    @pl.when(pl.program_id(1) == 0)
    def _():
        carry[...] = jnp.zeros(carry.shape, F32)

    hs_ref = _segment_rows_ref(h, seg_in, stage, hseg)
    xn = _rms_rows(hs_ref[...], g_ref[...]).astype(BF16)
    for c in range(F // FFN_CHUNK):
        gc = slice(c * FFN_CHUNK, (c + 1) * FFN_CHUNK)
        vc = slice(F + c * FFN_CHUNK, F + (c + 1) * FFN_CHUNK)
        gg = _segment_conv(_dot(xn, wup_ref[:, gc]), carry, gc, cw_ref, cb_ref, taps)
        vv = _segment_conv(_dot(xn, wup_ref[:, vc]), carry, vc, cw_ref, cb_ref, taps)
        act[:, gc] = (_gelu(gg) * vv).astype(BF16)
    _store_segment_rows(hs_ref[...] + _dot(act[...], wdn_ref[...]), seg_out, stage, o_ref)


def _stage_shape(tm, C):
    return (C // LANES, SUBLANES * _seg_pitch(tm // SUBLANES), LANES)


def _resident(shape):
    return pl.BlockSpec(shape, lambda b, i: (0,) * len(shape), pipeline_mode=pl.Buffered(1))


SEQ_TILE = 512


def _conv_ffn(h3, g, wup, cw, cb, wdn, attn=None, seg_in=False, seg_out=False, tm=SEQ_TILE):
    assert not (seg_in and attn is not None)
    B, S, D = h3.shape
    F2 = wup.shape[1]
    taps = cw.shape[0]
    row_tile = lambda width: pl.BlockSpec((None, tm, width), lambda b, i: (b, i, 0))
    attn_args, attn_specs = (), []
    if attn is not None:
        oa, ob, wo = attn
        attn_args = (oa, ob, wo)
        attn_specs = [row_tile(oa.shape[2]), row_tile(ob.shape[2]), _resident(wo.shape)]
    return pl.pallas_call(
        functools.partial(_ffn_kernel, taps=taps, with_attn=attn is not None,
                          seg_in=seg_in, seg_out=seg_out),
        grid=(B, S // tm),
        in_specs=[
            row_tile(D),
            *attn_specs,
            _resident((1, D)),
            _resident(wup.shape),
            _resident(cw.shape),
            _resident((1, F2)),
            _resident(wdn.shape),
        ],
        out_specs=row_tile(D),
        out_shape=jax.ShapeDtypeStruct((B, S, D), F32),
        scratch_shapes=[
            pltpu.VMEM(_stage_shape(tm, D), F32),
            pltpu.VMEM((tm, D), F32),
            pltpu.VMEM((taps - 1, SUBLANES, F2), F32),
            pltpu.VMEM((tm, F2 // 2), BF16),
        ],
        compiler_params=_params(("arbitrary", "arbitrary")),
        name="conv_ffn",
    )(h3, *attn_args, g, wup, cw, cb, wdn)


def _rec_kernel(h_ref, g_ref, win_ref, cw_ref, cb_ref, wa_ref, ba_ref, wx_ref, bx_ref,
                ap_ref, wout_ref, o_ref, stage, hseg, carry, hstate, *, taps, seg_in, seg_out):
    tm, C = h_ref.shape
    seg = tm // SUBLANES
    bw = C // LRU_BLOCKS

    @pl.when(pl.program_id(1) == 0)
    def _():
        carry[...] = jnp.zeros(carry.shape, F32)
        hstate[...] = jnp.zeros(hstate.shape, F32)

    hs_ref = _segment_rows_ref(h_ref, seg_in, stage, hseg)
    xn = _rms_rows(hs_ref[...], g_ref[...]).astype(BF16)
    gate = _dot(xn, win_ref[:, 0:C])
    xr_all = _segment_conv(_dot(xn, win_ref[:, C:]), carry, slice(0, C), cw_ref, cb_ref, taps)

    ap = ap_ref[...]
    decay = -LRU_C * (jnp.maximum(-ap, 0.0) + jnp.log1p(jnp.exp(-jnp.abs(ap))))
    sublane = lax.broadcasted_iota(jnp.int32, (SUBLANES, bw), 0)
    blocks = []
    for n in range(LRU_BLOCKS):
        cols = slice(n * bw, (n + 1) * bw)
        xr = xr_all[:, cols]
        xb = xr.astype(BF16)
        r = _sigmoid(_dot(xb, wa_ref[n]) + ba_ref[:, cols])
        i = _sigmoid(_dot(xb, wx_ref[n]) + bx_ref[:, cols])
        log_a = decay[:, cols] * r
        a = jnp.exp(log_a)
        u = jnp.sqrt(-jnp.tanh(log_a) * (1.0 + a * a)) * (i * xr)

        hl = jnp.zeros((SUBLANES, bw), F32)
        pp = jnp.ones((SUBLANES, bw), F32)
        hls, pps = [], []
        for j in range(seg):
            aj = a[j * SUBLANES:(j + 1) * SUBLANES]
            hl = aj * hl + u[j * SUBLANES:(j + 1) * SUBLANES]
            pp = aj * pp
            hls.append(hl)
            pps.append(pp)
        cin = hstate[0:1, cols]
        h_in = jnp.zeros((SUBLANES, bw), F32)
        for s in range(SUBLANES):
            h_in = jnp.where(sublane == s, cin, h_in)
            cin = hl[s:s + 1] + pp[s:s + 1] * cin
        hstate[:, cols] = jnp.broadcast_to(cin, (SUBLANES, bw))
        blocks.append(jnp.concatenate([hls[j] + pps[j] * h_in for j in range(seg)], axis=0))

    hs = jnp.concatenate(blocks, axis=1)
    y = (hs * _gelu(gate)).astype(BF16)
    _store_segment_rows(hs_ref[...] + _dot(y, wout_ref[...]), seg_out, stage, o_ref)


def _recurrent_block(h3, g, win, cw, cb, wa, ba, wx, bx, ap, wout,
                     seg_in=False, seg_out=False, tm=SEQ_TILE):
    B, S, D = h3.shape
    C = wout.shape[0]
    taps = cw.shape[0]
    unused = (SUBLANES, LANES)
    stage_shape = unused if (seg_in and seg_out) else _stage_shape(tm, D)
    hseg_shape = unused if seg_in else (tm, D)
    return pl.pallas_call(
        functools.partial(_rec_kernel, taps=taps, seg_in=seg_in, seg_out=seg_out),
        grid=(B, S // tm),
        in_specs=[
            pl.BlockSpec((None, tm, D), lambda b, i: (b, i, 0)),
            _resident((1, D)), _resident(win.shape), _resident(cw.shape), _resident((1, C)),
            _resident(wa.shape), _resident((1, C)), _resident(wx.shape), _resident((1, C)),
            _resident((1, C)), _resident(wout.shape),
        ],
        out_specs=pl.BlockSpec((None, tm, D), lambda b, i: (b, i, 0)),
        out_shape=jax.ShapeDtypeStruct((B, S, D), F32),
        scratch_shapes=[
            pltpu.VMEM(stage_shape, F32),
            pltpu.VMEM(hseg_shape, F32),
            pltpu.VMEM((taps - 1, SUBLANES, C), F32),
            pltpu.VMEM((SUBLANES, C), F32),
        ],
        compiler_params=_params(("arbitrary", "arbitrary")),
        name="recurrent_block",
    )(h3, g, win, cw, cb, wa, ba, wx, bx, ap, wout)


def _alibi_slopes(n):
    return jnp.exp2(-8.0 * jnp.arange(1, n + 1, dtype=F32) / n)


def _row(v):
    return v.reshape(1, -1).astype(F32)


def kernel(x, attn_norm, attn_w_in, attn_w_out, a_q_norm, a_k_norm, b_q_norm, b_k_norm, b_sub_norm,
           b_lam_q1, b_lam_k1, b_lam_q2, b_lam_k2, rec_norm, rec_w_in, rec_conv_w, rec_conv_b,
           rec_gate_a_w, rec_gate_a_b, rec_gate_x_w, rec_gate_x_b, rec_a_param, rec_w_out,
           ffn_norm, ffn_w_up, ffn_conv_w, ffn_conv_b, ffn_w_down):
    B, S, D = x.shape
    depth = ffn_norm.shape[0]
    slopes = _alibi_slopes(A_HEADS + B_HEADS)
    h = x
    h_is_seg = False
    for layer in range(depth):
        j = layer // 2
        attn = None
        if layer % 2 == 0:
            lam_init = 0.8 - 0.6 * math.exp(-0.3 * layer)
            scale = HEAD_DIM ** -0.5
            reps = SEC // HEAD_DIM
            ones = jnp.ones((SEC,), F32)
            head_gains = jnp.stack([
                jnp.tile(a_q_norm[j].astype(F32), reps) * (scale * LOG2E),
                jnp.tile(a_k_norm[j].astype(F32), reps), ones,
                jnp.tile(b_q_norm[j].astype(F32), reps) * (scale * LOG2E),
                jnp.tile(b_k_norm[j].astype(F32), reps), ones])
            proj = _attn_inproj(h.reshape(B * S, D), _row(attn_norm[j]),
                                attn_w_in[j].astype(BF16), head_gains)
            proj3 = proj.reshape(B, S, -1)
            oa = _dilated_attention(proj3, slopes[:A_HEADS] * LOG2E)
            lam_vecs = jnp.stack([b_lam_q1[j], b_lam_k1[j], b_lam_q2[j], b_lam_k2[j]]).astype(F32)
            ob = _diff_attention(proj3, slopes[A_HEADS:] * LOG2E, lam_vecs, _row(b_sub_norm[j]),
                                 lam_init)
            attn = (oa, ob, attn_w_out[j].astype(BF16))
        else:
            h = _recurrent_block(
                h, _row(rec_norm[j]), rec_w_in[j].astype(BF16), rec_conv_w[j].astype(F32),
                _row(rec_conv_b[j]), rec_gate_a_w[j].astype(BF16), _row(rec_gate_a_b[j]),
                rec_gate_x_w[j].astype(BF16), _row(rec_gate_x_b[j]), _row(rec_a_param[j]),
                rec_w_out[j].astype(BF16), seg_in=h_is_seg, seg_out=True)
            h_is_seg = True
        next_is_rec = layer + 1 < depth and (layer + 1) % 2 == 1
        h = _conv_ffn(h, _row(ffn_norm[layer]), ffn_w_up[layer].astype(BF16),
                      ffn_conv_w[layer].astype(F32), _row(ffn_conv_b[layer]),
                      ffn_w_down[layer].astype(BF16), attn=attn,
                      seg_in=h_is_seg, seg_out=next_is_rec)
        h_is_seg = next_is_rec
    return h
```

```python
import functools
import math

import numpy as np
import jax
import jax.numpy as jnp
from jax import lax
from jax.experimental import pallas as pl
from jax.experimental.pallas import tpu as pltpu

F32 = jnp.float32
BF16 = jnp.bfloat16

HEAD_DIM = 64
A_HEADS = 8
B_HEADS = 4
DILATED_CONFIGS = ((128, 1), (512, 4), (2048, 16))
BAND = 128
MAX_WINDOW = 2048
LRU_BLOCKS = 8
LRU_C = 8.0
NORM_EPS = 1e-6
NEG_INF = -1e30
LOG2E = math.log2(math.e)
LANES = 128
SUBLANES = 8
VMEM_LIMIT = 56 * 1024 * 1024


def _gelu(x):
    c = math.sqrt(2.0 / math.pi)
    return x * (0.5 * (1.0 + jnp.tanh(c * (x + 0.044715 * (x * x * x)))))


def _sigmoid(x):
    return 1.0 / (1.0 + jnp.exp(-x))


def _rms_rows(x, g):
    ms = jnp.mean(x * x, axis=-1, keepdims=True)
    return x * lax.rsqrt(ms + NORM_EPS) * g


def _dot(a, b):
    return jnp.dot(a, b, preferred_element_type=F32)


def _dot_nt(a, b):
    return lax.dot_general(a, b, (((1,), (1,)), ((), ())), preferred_element_type=F32)


def _params(sem):
    return pltpu.CompilerParams(dimension_semantics=sem, vmem_limit_bytes=VMEM_LIMIT)


SEC = 512
N_SEC = 6
MXU_TILE = 256


def _inproj_kernel(x_ref, g_ref, w_ref, hg_ref, p_ref, o_ref):
    xn = _rms_rows(x_ref[...], g_ref[...]).astype(BF16)
    for s in range(N_SEC):
        y = _dot(xn, w_ref[:, s * SEC:(s + 1) * SEC])
        if s % 3 == 2:
            o_ref[:, s * SEC:(s + 1) * SEC] = y.astype(BF16)
        else:
            y2 = (y * y).astype(BF16)
            ms = jnp.concatenate(
                [_dot(y2[:, c:c + MXU_TILE], p_ref[...]) for c in range(0, SEC, MXU_TILE)], axis=1)
            o_ref[:, s * SEC:(s + 1) * SEC] = (
                y * lax.rsqrt(ms + NORM_EPS) * hg_ref[s:s + 1, :]).astype(BF16)


def _attn_inproj(x2, g, w, head_gains, tm=512):
    T, D = x2.shape
    N = w.shape[1]
    blk = np.kron(np.eye(MXU_TILE // HEAD_DIM), np.full((HEAD_DIM, HEAD_DIM), 1.0 / HEAD_DIM))
    pmat = jnp.asarray(blk, dtype=BF16)
    return pl.pallas_call(
        _inproj_kernel,
        grid=(T // tm,),
        in_specs=[
            pl.BlockSpec((tm, D), lambda i: (i, 0)),
            pl.BlockSpec((1, D), lambda i: (0, 0)),
            pl.BlockSpec((D, N), lambda i: (0, 0)),
            pl.BlockSpec((N_SEC, SEC), lambda i: (0, 0)),
            pl.BlockSpec((MXU_TILE, MXU_TILE), lambda i: (0, 0)),
        ],
        out_specs=pl.BlockSpec((tm, N), lambda i: (i, 0)),
        out_shape=jax.ShapeDtypeStruct((T, N), BF16),
        compiler_params=_params(("parallel",)),
        name="attn_inproj",
    )(x2, g, w, head_gains, pmat)


ROW_GROUP = 16
ROW_PITCH = 24


def _spread_row(t):
    return (t // ROW_GROUP) * ROW_PITCH + t % ROW_GROUP


def _dilated_kernel(slopes_ref, q_ref, k_ref, v_ref, o_ref, qf, kf, vf, qp, kp, vp, ob, db, mb,
                    sb, bias_buf):
    assert [d % ROW_GROUP == 0 for _, d in DILATED_CONFIGS] == [False, False, True]
    S = q_ref.shape[0]
    pad = kf.shape[0] - S
    g = pl.program_id(1)
    qf[...] = q_ref[...].astype(F32)
    kf[0:pad, :] = jnp.zeros((pad, LANES), F32)
    vf[0:pad, :] = jnp.zeros((pad, LANES), F32)
    kf[pad:, :] = k_ref[...].astype(F32)
    vf[pad:, :] = v_ref[...].astype(F32)
    kp[0:_spread_row(pad), :] = jnp.zeros((_spread_row(pad), LANES), F32)
    vp[0:_spread_row(pad), :] = jnp.zeros((_spread_row(pad), LANES), F32)

    def spread(grp, carry):
        src = pl.ds(pl.multiple_of(grp * ROW_GROUP, ROW_GROUP), ROW_GROUP)
        dst = pl.ds(pl.multiple_of(grp * ROW_PITCH, SUBLANES), ROW_GROUP)
        dst_kv = pl.ds(pl.multiple_of(grp * ROW_PITCH + _spread_row(pad), SUBLANES), ROW_GROUP)
        qp[dst, :] = q_ref[src, :].astype(F32)
        kp[dst_kv, :] = k_ref[src, :].astype(F32)
        vp[dst_kv, :] = v_ref[src, :].astype(F32)
        return carry
    lax.fori_loop(0, S // ROW_GROUP, spread, 0, unroll=8)

    lo = lax.broadcasted_iota(jnp.int32, (BAND, LANES), 1) < HEAD_DIM
    ii = lax.broadcasted_iota(jnp.int32, (BAND, 2 * BAND), 0)
    jj = lax.broadcasted_iota(jnp.int32, (BAND, 2 * BAND), 1)
    delta = ii + BAND - jj
    in_band = (delta >= 0) & (delta <= BAND)
    prev_half = lax.broadcasted_iota(jnp.int32, (2 * BAND, 2 * BAND), 1) < BAND
    sl0 = slopes_ref[2 * g]
    sl1 = slopes_ref[2 * g + 1]
    ones = jnp.ones((2 * BAND, LANES), BF16)

    for bi, (window, d) in enumerate(DILATED_CONFIGS):
        span = BAND * d
        nbs = S // span
        dist = (delta * d).astype(F32)
        bias = jnp.concatenate(
            [jnp.where(in_band, -sl0 * dist, NEG_INF),
             jnp.where(in_band, -sl1 * dist, NEG_INF)], axis=0)
        bias_buf[2 * bi] = bias
        bias_buf[2 * bi + 1] = jnp.where(prev_half, NEG_INF, bias)

        def body(blk, carry, d=d, span=span, nbs=nbs, bi=bi):
            r = blk // nbs
            nb = blk - r * nbs
            start = nb * span + r
            if d % ROW_GROUP == 0:
                stride = d // ROW_GROUP * ROW_PITCH
                q_row = nb * _spread_row(span) + r
                kv_row = q_row + _spread_row(pad) - _spread_row(span)
                q = qp[pl.ds(q_row, BAND, stride=stride), :].astype(BF16)
                k = kp[pl.ds(kv_row, 2 * BAND, stride=stride), :].astype(BF16)
                v = vp[pl.ds(kv_row, 2 * BAND, stride=stride), :].astype(BF16)
            else:
                q = qf[pl.ds(start, BAND, stride=d), :].astype(BF16)
                k = kf[pl.ds(pad + start - span, 2 * BAND, stride=d), :].astype(BF16)
                v = vf[pl.ds(pad + start - span, 2 * BAND, stride=d), :].astype(BF16)
            zero = jnp.zeros_like(q)
            q2 = jnp.concatenate([jnp.where(lo, q, zero), jnp.where(lo, zero, q)], axis=0)
            s = _dot_nt(q2, k) + bias_buf[2 * bi + jnp.where(nb == 0, 1, 0)]
            m = jnp.max(s, axis=1, keepdims=True)
            e = jnp.exp2(s - m)
            pv = _dot(e.astype(BF16), jnp.concatenate([v, ones], axis=1))
            num = jnp.where(lo, pv[:BAND, :LANES], pv[BAND:, :LANES])
            den = jnp.where(lo, pv[:BAND, LANES:], pv[BAND:, LANES:])
            top = jnp.where(lo, jnp.broadcast_to(m[:BAND], (BAND, LANES)),
                            jnp.broadcast_to(m[BAND:], (BAND, LANES)))
            if d % ROW_GROUP == 0:
                rows = pl.ds(q_row, BAND, stride=stride)
                sb[0, rows, :] = num
                sb[1, rows, :] = den
                sb[2, rows, :] = top
            else:
                rows = pl.ds(start, BAND, stride=d)
                ob[bi, rows, :] = num
                db[bi, rows, :] = den
                mb[bi, rows, :] = top
            return carry

        lax.fori_loop(0, S // BAND, body, 0, unroll=16)

    rows = 512
    def mix(c, carry):
        sl = pl.ds(pl.multiple_of(c * rows, rows), rows)
        base = pl.multiple_of(c * _spread_row(rows), SUBLANES)

        def spread_chunk(t):
            return jnp.concatenate(
                [sb[t, pl.ds(base + gi * ROW_PITCH, ROW_GROUP), :] for gi in range(rows // ROW_GROUP)],
                axis=0)
        m0, m1, m2 = mb[0, sl, :], mb[1, sl, :], spread_chunk(2)
        m = jnp.maximum(jnp.maximum(m0, m1), m2)
        w0, w1, w2 = jnp.exp2(m0 - m), jnp.exp2(m1 - m), jnp.exp2(m2 - m)
        num = w0 * ob[0, sl, :] + w1 * ob[1, sl, :] + w2 * spread_chunk(0)
        den = w0 * db[0, sl, :] + w1 * db[1, sl, :] + w2 * spread_chunk(1)
        o_ref[sl, :] = (num / den).astype(o_ref.dtype)
        return carry
    lax.fori_loop(0, S // rows, mix, 0)


def _dilated_attention(proj3, slopes):
    B, S, _ = proj3.shape
    assert S % MAX_WINDOW == 0
    pairs = A_HEADS // 2
    blk = lambda off: pl.BlockSpec((None, S, LANES), lambda b, g, off=off: (b, 0, off + g))
    return pl.pallas_call(
        _dilated_kernel,
        grid=(B, pairs),
        in_specs=[
            pl.BlockSpec(memory_space=pltpu.SMEM),
            blk(0), blk(pairs), blk(2 * pairs),
        ],
        out_specs=pl.BlockSpec((None, S, LANES), lambda b, g: (b, 0, g)),
        out_shape=jax.ShapeDtypeStruct((B, S, A_HEADS * HEAD_DIM), BF16),
        scratch_shapes=[
            pltpu.VMEM((S, LANES), F32),
            pltpu.VMEM((MAX_WINDOW + S, LANES), F32),
            pltpu.VMEM((MAX_WINDOW + S, LANES), F32),
            pltpu.VMEM((_spread_row(S), LANES), F32),
            pltpu.VMEM((_spread_row(MAX_WINDOW + S), LANES), F32),
            pltpu.VMEM((_spread_row(MAX_WINDOW + S), LANES), F32),
            pltpu.VMEM((len(DILATED_CONFIGS) - 1, S, LANES), F32),
            pltpu.VMEM((len(DILATED_CONFIGS) - 1, S, LANES), F32),
            pltpu.VMEM((len(DILATED_CONFIGS) - 1, S, LANES), F32),
            pltpu.VMEM((3, _spread_row(S), LANES), F32),
            pltpu.VMEM((2 * len(DILATED_CONFIGS), 2 * BAND, 2 * BAND), F32),
        ],
        compiler_params=_params(("parallel", "parallel")),
        name="dilated_attn",
    )(slopes, proj3, proj3, proj3)


TQ = 512
KV_UNROLL = 2
DIFF_HEADS = 4


def _diff_kernel(slopes_ref, q_ref, k_ref, v_ref, lam_ref, sg_ref, o_ref,
                 m_ref, acc_ref, *, lam_init):
    hg = pl.program_id(1)
    qi = pl.program_id(2)
    lo = lax.broadcasted_iota(jnp.int32, (TQ, LANES), 1) < HEAD_DIM
    heads = range(DIFF_HEADS)
    cols = [slice(hh * LANES, (hh + 1) * LANES) for hh in heads]
    slopes = [slopes_ref[hg * DIFF_HEADS + hh] for hh in heads]
    q2 = []
    for hh in heads:
        q = q_ref[:, cols[hh]]
        zero = jnp.zeros_like(q)
        q2.append(jnp.concatenate([jnp.where(lo, q, zero), jnp.where(lo, zero, q)], axis=0))

    m_ref[...] = jnp.full(m_ref.shape, NEG_INF, F32)
    acc_ref[...] = jnp.zeros(acc_ref.shape, F32)

    def update(hh, key0, nk, row_blocks, masked):
        ks = pl.ds(pl.multiple_of(key0, nk), nk)
        k = k_ref[ks, cols[hh]]
        v1 = jnp.concatenate([v_ref[ks, cols[hh]], jnp.ones((nk, LANES), BF16)], axis=1)
        qs = jnp.concatenate([q2[hh][r0:r0 + rn] for r0, rn in row_blocks], axis=0)
        nrows = qs.shape[0]
        kpos = lax.broadcasted_iota(jnp.int32, (1, nk), 1).astype(F32) + key0.astype(F32)
        s = _dot_nt(qs, k) + slopes[hh] * kpos
        if masked:
            rn = row_blocks[0][1]
            rel = (lax.broadcasted_iota(jnp.int32, (nrows, nk), 0) % rn
                   - lax.broadcasted_iota(jnp.int32, (nrows, nk), 1))
            s = jnp.where(rel >= 0, s, NEG_INF)
        m_old = jnp.concatenate([m_ref[hh, r0:r0 + rn] for r0, rn in row_blocks], axis=0)
        a_old = jnp.concatenate([acc_ref[hh, r0:r0 + rn] for r0, rn in row_blocks], axis=0)
        m_new = jnp.maximum(m_old, jnp.max(s, axis=1, keepdims=True))
        alpha = jnp.exp2(m_old - m_new)
        e = jnp.exp2(s - jnp.concatenate([m_new] * (nk // LANES), axis=1))
        a_new = jnp.concatenate([alpha, alpha], axis=1) * a_old + _dot(e.astype(BF16), v1)
        at = 0
        for r0, rn in row_blocks:
            m_ref[hh, r0:r0 + rn] = m_new[at:at + rn]
            acc_ref[hh, r0:r0 + rn] = a_new[at:at + rn]
            at += rn

    half = TQ // 2

    def step(j, masked, hh):
        if not masked:
            update(hh, j * TQ, TQ, [(0, 2 * TQ)], False)
            return
        update(hh, j * TQ, half, [(0, TQ), (TQ, TQ)], True)
        update(hh, j * TQ + half, half, [(half, half), (TQ + half, half)], True)

    def body(jq, carry):
        for u in range(KV_UNROLL):
            for hh in heads:
                step(KV_UNROLL * jq + u, False, hh)
        return carry
    nq = qi // KV_UNROLL
    lax.fori_loop(0, nq, body, 0)

    rem = qi - nq * KV_UNROLL
    for r in range(KV_UNROLL):
        @pl.when(rem == r)
        def _(r=r):
            for u in range(r):
                for hh in heads:
                    step(qi - r + u, False, hh)
            for hh in heads:
                step(qi, True, hh)

    lq = lam_ref[...]
    lam = (jnp.exp(jnp.sum(lq[0:1] * lq[1:2], axis=1, keepdims=True))
           - jnp.exp(jnp.sum(lq[2:3] * lq[3:4], axis=1, keepdims=True)) + lam_init)
    for hh in heads:
        on = acc_ref[hh, :, 0:LANES] / acc_ref[hh, :, LANES:]
        o = on[:TQ] - lam * on[TQ:]
        o = _rms_rows(o, sg_ref[...]) * (1.0 - lam_init)
        o_ref[:, cols[hh]] = o.astype(o_ref.dtype)


def _diff_attention(proj3, slopes, lam_vecs, sub_gain, lam_init):
    B, S, _ = proj3.shape
    width = DIFF_HEADS * LANES
    groups = B_HEADS // DIFF_HEADS
    qoff = 3 * A_HEADS * HEAD_DIM // width
    koff = qoff + groups
    voff = koff + groups
    return pl.pallas_call(
        functools.partial(_diff_kernel, lam_init=lam_init),
        grid=(B, groups, S // TQ),
        in_specs=[
            pl.BlockSpec(memory_space=pltpu.SMEM),
            pl.BlockSpec((None, TQ, width), lambda b, h, i: (b, i, qoff + h)),
            pl.BlockSpec((None, S, width), lambda b, h, i: (b, 0, koff + h)),
            pl.BlockSpec((None, S, width), lambda b, h, i: (b, 0, voff + h)),
            pl.BlockSpec((4, HEAD_DIM), lambda b, h, i: (0, 0)),
            pl.BlockSpec((1, 2 * HEAD_DIM), lambda b, h, i: (0, 0)),
        ],
        out_specs=pl.BlockSpec((None, TQ, width), lambda b, h, i: (b, i, h)),
        out_shape=jax.ShapeDtypeStruct((B, S, B_HEADS * 2 * HEAD_DIM), BF16),
        scratch_shapes=[
            pltpu.VMEM((DIFF_HEADS, 2 * TQ, LANES), F32),
            pltpu.VMEM((DIFF_HEADS, 2 * TQ, 2 * LANES), F32),
        ],
        compiler_params=_params(("parallel", "parallel", "parallel")),
        name="diff_attn",
    )(slopes, proj3, proj3, proj3, lam_vecs, sub_gain)


FFN_CHUNK = 512


def _seg_pitch(seg):
    p = seg // SUBLANES + 1
    return SUBLANES * (p if p % 2 else p + 1)


def _to_segment_rows(src, stage, dst_ref):
    tm, C = src.shape
    seg = tm // SUBLANES
    pitch = _seg_pitch(seg)
    for n in range(C // LANES):
        cols = slice(n * LANES, (n + 1) * LANES)
        for s in range(SUBLANES):
            stage[n, s * pitch:s * pitch + seg, :] = src[s * seg:(s + 1) * seg, cols]
    for j in range(seg):
        dst_ref[j * SUBLANES:(j + 1) * SUBLANES, :] = jnp.concatenate(
            [stage[n, pl.ds(j, SUBLANES, stride=pitch), :] for n in range(C // LANES)], axis=1)


def _from_segment_rows(val, stage, dst_ref):
    tm, C = val.shape
    seg = tm // SUBLANES
    pitch = _seg_pitch(seg)
    for n in range(C // LANES):
        cols = slice(n * LANES, (n + 1) * LANES)
        for j in range(seg):
            stage[n, pl.ds(j, SUBLANES, stride=pitch), :] = val[j * SUBLANES:(j + 1) * SUBLANES, cols]
        for s in range(SUBLANES):
            dst_ref[s * seg:(s + 1) * seg, cols] = stage[n, s * pitch:s * pitch + seg, :]


def _segment_rows_ref(src, seg_in, stage, hseg):
    if seg_in:
        return src
    _to_segment_rows(src, stage, hseg)
    return hseg


def _store_segment_rows(val, seg_out, stage, o_ref):
    if seg_out:
        o_ref[...] = val
    else:
        _from_segment_rows(val, stage, o_ref)


def _segment_conv(u, carry_ref, cols, w_ref, b_ref, taps):
    tm = u.shape[0]
    first_sublane = lax.broadcasted_iota(jnp.int32, (SUBLANES, u.shape[1]), 0) == 0
    wrapped = []
    for i in range(1, taps):
        tail = u[tm - i * SUBLANES:tm - (i - 1) * SUBLANES]
        wrapped.append(jnp.where(first_sublane,
                                 pltpu.roll(carry_ref[i - 1, :, cols], 1, 0),
                                 pltpu.roll(tail, 1, 0)))
        carry_ref[i - 1, :, cols] = tail
    out = b_ref[:, cols] + w_ref[taps - 1:taps, cols] * u
    for k in range(1, taps):
        shifted = jnp.concatenate(wrapped[k - 1::-1] + [u[:tm - k * SUBLANES]], axis=0)
        out = out + w_ref[taps - 1 - k:taps - k, cols] * shifted
    return out


def _ffn_kernel(*refs, taps, with_attn, seg_in, seg_out):
    if with_attn:
        x_ref, a_ref, b_ref, wo_ref, *refs = refs
        na = a_ref.shape[1]
        h = x_ref[...] + _dot(a_ref[...], wo_ref[0:na, :]) + _dot(b_ref[...], wo_ref[na:, :])
    else:
        h, *refs = refs
    g_ref, wup_ref, cw_ref, cb_ref, wdn_ref, o_ref, stage, hseg, carry, act = refs
    F = wdn_ref.shape[0]

    @pl.when(pl.program_id(1) == 0)
    def _():
        carry[...] = jnp.zeros(carry.shape, F32)

    hs_ref = _segment_rows_ref(h, seg_in, stage, hseg)
    xn = _rms_rows(hs_ref[...], g_ref[...]).astype(BF16)
    for c in range(F // FFN_CHUNK):
        gc = slice(c * FFN_CHUNK, (c + 1) * FFN_CHUNK)
        vc = slice(F + c * FFN_CHUNK, F + (c + 1) * FFN_CHUNK)
        gg = _segment_conv(_dot(xn, wup_ref[:, gc]), carry, gc, cw_ref, cb_ref, taps)
        vv = _segment_conv(_dot(xn, wup_ref[:, vc]), carry, vc, cw_ref, cb_ref, taps)
        act[:, gc] = (_gelu(gg) * vv).astype(BF16)
    _store_segment_rows(hs_ref[...] + _dot(act[...], wdn_ref[...]), seg_out, stage, o_ref)


def _stage_shape(tm, C):
    return (C // LANES, SUBLANES * _seg_pitch(tm // SUBLANES), LANES)


def _resident(shape):
    return pl.BlockSpec(shape, lambda b, i: (0,) * len(shape), pipeline_mode=pl.Buffered(1))


SEQ_TILE = 512


def _conv_ffn(h3, g, wup, cw, cb, wdn, attn=None, seg_in=False, seg_out=False, tm=SEQ_TILE):
    assert not (seg_in and attn is not None)
    B, S, D = h3.shape
    F2 = wup.shape[1]
    taps = cw.shape[0]
    row_tile = lambda width: pl.BlockSpec((None, tm, width), lambda b, i: (b, i, 0))
    attn_args, attn_specs = (), []
    if attn is not None:
        oa, ob, wo = attn
        attn_args = (oa, ob, wo)
        attn_specs = [row_tile(oa.shape[2]), row_tile(ob.shape[2]), _resident(wo.shape)]
    return pl.pallas_call(
        functools.partial(_ffn_kernel, taps=taps, with_attn=attn is not None,
                          seg_in=seg_in, seg_out=seg_out),
        grid=(B, S // tm),
        in_specs=[
            row_tile(D),
            *attn_specs,
            _resident((1, D)),
            _resident(wup.shape),
            _resident(cw.shape),
            _resident((1, F2)),
            _resident(wdn.shape),
        ],
        out_specs=row_tile(D),
        out_shape=jax.ShapeDtypeStruct((B, S, D), F32),
        scratch_shapes=[
            pltpu.VMEM(_stage_shape(tm, D), F32),
            pltpu.VMEM((tm, D), F32),
            pltpu.VMEM((taps - 1, SUBLANES, F2), F32),
            pltpu.VMEM((tm, F2 // 2), BF16),
        ],
        compiler_params=_params(("arbitrary", "arbitrary")),
        name="conv_ffn",
    )(h3, *attn_args, g, wup, cw, cb, wdn)


def _rec_kernel(h_ref, g_ref, win_ref, cw_ref, cb_ref, wa_ref, ba_ref, wx_ref, bx_ref,
                ap_ref, wout_ref, o_ref, stage, hseg, carry, hstate, *, taps, seg_in, seg_out):
    tm, C = h_ref.shape
    seg = tm // SUBLANES
    bw = C // LRU_BLOCKS

    @pl.when(pl.program_id(1) == 0)
    def _():
        carry[...] = jnp.zeros(carry.shape, F32)
        hstate[...] = jnp.zeros(hstate.shape, F32)

    hs_ref = _segment_rows_ref(h_ref, seg_in, stage, hseg)
    xn = _rms_rows(hs_ref[...], g_ref[...]).astype(BF16)
    gate = _dot(xn, win_ref[:, 0:C])
    xr_all = _segment_conv(_dot(xn, win_ref[:, C:]), carry, slice(0, C), cw_ref, cb_ref, taps)

    ap = ap_ref[...]
    decay = -LRU_C * (jnp.maximum(-ap, 0.0) + jnp.log1p(jnp.exp(-jnp.abs(ap))))
    sublane = lax.broadcasted_iota(jnp.int32, (SUBLANES, bw), 0)
    blocks = []
    for n in range(LRU_BLOCKS):
        cols = slice(n * bw, (n + 1) * bw)
        xr = xr_all[:, cols]
        xb = xr.astype(BF16)
        r = _sigmoid(_dot(xb, wa_ref[n]) + ba_ref[:, cols])
        i = _sigmoid(_dot(xb, wx_ref[n]) + bx_ref[:, cols])
        log_a = decay[:, cols] * r
        a = jnp.exp(log_a)
        u = jnp.sqrt(-jnp.tanh(log_a) * (1.0 + a * a)) * (i * xr)

        hl = jnp.zeros((SUBLANES, bw), F32)
        pp = jnp.ones((SUBLANES, bw), F32)
        hls, pps = [], []
        for j in range(seg):
            aj = a[j * SUBLANES:(j + 1) * SUBLANES]
            hl = aj * hl + u[j * SUBLANES:(j + 1) * SUBLANES]
            pp = aj * pp
            hls.append(hl)
            pps.append(pp)
        cin = hstate[0:1, cols]
        h_in = jnp.zeros((SUBLANES, bw), F32)
        for s in range(SUBLANES):
            h_in = jnp.where(sublane == s, cin, h_in)
            cin = hl[s:s + 1] + pp[s:s + 1] * cin
        hstate[:, cols] = jnp.broadcast_to(cin, (SUBLANES, bw))
        blocks.append(jnp.concatenate([hls[j] + pps[j] * h_in for j in range(seg)], axis=0))

    hs = jnp.concatenate(blocks, axis=1)
    y = (hs * _gelu(gate)).astype(BF16)
    _store_segment_rows(hs_ref[...] + _dot(y, wout_ref[...]), seg_out, stage, o_ref)


def _recurrent_block(h3, g, win, cw, cb, wa, ba, wx, bx, ap, wout,
                     seg_in=False, seg_out=False, tm=SEQ_TILE):
    B, S, D = h3.shape
    C = wout.shape[0]
    taps = cw.shape[0]
    unused = (SUBLANES, LANES)
    stage_shape = unused if (seg_in and seg_out) else _stage_shape(tm, D)
    hseg_shape = unused if seg_in else (tm, D)
    return pl.pallas_call(
        functools.partial(_rec_kernel, taps=taps, seg_in=seg_in, seg_out=seg_out),
        grid=(B, S // tm),
        in_specs=[
            pl.BlockSpec((None, tm, D), lambda b, i: (b, i, 0)),
            _resident((1, D)), _resident(win.shape), _resident(cw.shape), _resident((1, C)),
            _resident(wa.shape), _resident((1, C)), _resident(wx.shape), _resident((1, C)),
            _resident((1, C)), _resident(wout.shape),
        ],
        out_specs=pl.BlockSpec((None, tm, D), lambda b, i: (b, i, 0)),
        out_shape=jax.ShapeDtypeStruct((B, S, D), F32),
        scratch_shapes=[
            pltpu.VMEM(stage_shape, F32),
            pltpu.VMEM(hseg_shape, F32),
            pltpu.VMEM((taps - 1, SUBLANES, C), F32),
            pltpu.VMEM((SUBLANES, C), F32),
        ],
        compiler_params=_params(("arbitrary", "arbitrary")),
        name="recurrent_block",
    )(h3, g, win, cw, cb, wa, ba, wx, bx, ap, wout)


def _alibi_slopes(n):
    return jnp.exp2(-8.0 * jnp.arange(1, n + 1, dtype=F32) / n)


def _row(v):
    return v.reshape(1, -1).astype(F32)


def kernel(x, attn_norm, attn_w_in, attn_w_out, a_q_norm, a_k_norm, b_q_norm, b_k_norm, b_sub_norm,
           b_lam_q1, b_lam_k1, b_lam_q2, b_lam_k2, rec_norm, rec_w_in, rec_conv_w, rec_conv_b,
           rec_gate_a_w, rec_gate_a_b, rec_gate_x_w, rec_gate_x_b, rec_a_param, rec_w_out,
           ffn_norm, ffn_w_up, ffn_conv_w, ffn_conv_b, ffn_w_down):
    B, S, D = x.shape
    depth = ffn_norm.shape[0]
    slopes = _alibi_slopes(A_HEADS + B_HEADS)
    h = x
    h_is_seg = False
    for layer in range(depth):
        j = layer // 2
        attn = None
        if layer % 2 == 0:
            lam_init = 0.8 - 0.6 * math.exp(-0.3 * layer)
            scale = HEAD_DIM ** -0.5
            reps = SEC // HEAD_DIM
            ones = jnp.ones((SEC,), F32)
            head_gains = jnp.stack([
                jnp.tile(a_q_norm[j].astype(F32), reps) * (scale * LOG2E),
                jnp.tile(a_k_norm[j].astype(F32), reps), ones,
                jnp.tile(b_q_norm[j].astype(F32), reps) * (scale * LOG2E),
                jnp.tile(b_k_norm[j].astype(F32), reps), ones])
            proj = _attn_inproj(h.reshape(B * S, D), _row(attn_norm[j]),
                                attn_w_in[j].astype(BF16), head_gains)
            proj3 = proj.reshape(B, S, -1)
            oa = _dilated_attention(proj3, slopes[:A_HEADS] * LOG2E)
            lam_vecs = jnp.stack([b_lam_q1[j], b_lam_k1[j], b_lam_q2[j], b_lam_k2[j]]).astype(F32)
            ob = _diff_attention(proj3, slopes[A_HEADS:] * LOG2E, lam_vecs, _row(b_sub_norm[j]),
                                 lam_init)
            attn = (oa, ob, attn_w_out[j].astype(BF16))
        else:
            h = _recurrent_block(
                h, _row(rec_norm[j]), rec_w_in[j].astype(BF16), rec_conv_w[j].astype(F32),
                _row(rec_conv_b[j]), rec_gate_a_w[j].astype(BF16), _row(rec_gate_a_b[j]),
                rec_gate_x_w[j].astype(BF16), _row(rec_gate_x_b[j]), _row(rec_a_param[j]),
                rec_w_out[j].astype(BF16), seg_in=h_is_seg, seg_out=True)
            h_is_seg = True
        next_is_rec = layer + 1 < depth and (layer + 1) % 2 == 1
        h = _conv_ffn(h, _row(ffn_norm[layer]), ffn_w_up[layer].astype(BF16),
                      ffn_conv_w[layer].astype(F32), _row(ffn_conv_b[layer]),
                      ffn_w_down[layer].astype(BF16), attn=attn,
                      seg_in=h_is_seg, seg_out=next_is_rec)
        h_is_seg = next_is_rec
    return h
```

```python
import functools
import math

import numpy as np
import jax
import jax.numpy as jnp
from jax import lax
from jax.experimental import pallas as pl
from jax.experimental.pallas import tpu as pltpu

F32 = jnp.float32
BF16 = jnp.bfloat16

HEAD_DIM = 64
A_HEADS = 8
B_HEADS = 4
DILATED_CONFIGS = ((128, 1), (512, 4), (2048, 16))
BAND = 128
MAX_WINDOW = 2048
LRU_BLOCKS = 8
LRU_C = 8.0
NORM_EPS = 1e-6
NEG_INF = -1e30
LOG2E = math.log2(math.e)
LANES = 128
SUBLANES = 8
VMEM_LIMIT = 56 * 1024 * 1024


def _gelu(x):
    c = math.sqrt(2.0 / math.pi)
    return x * (0.5 * (1.0 + jnp.tanh(c * (x + 0.044715 * (x * x * x)))))


def _sigmoid(x):
    return 1.0 / (1.0 + jnp.exp(-x))


def _rms_rows(x, g):
    ms = jnp.mean(x * x, axis=-1, keepdims=True)
    return x * lax.rsqrt(ms + NORM_EPS) * g


def _dot(a, b):
    return jnp.dot(a, b, preferred_element_type=F32)


def _dot_nt(a, b):
    return lax.dot_general(a, b, (((1,), (1,)), ((), ())), preferred_element_type=F32)


def _params(sem):
    return pltpu.CompilerParams(dimension_semantics=sem, vmem_limit_bytes=VMEM_LIMIT)


SEC = 512
N_SEC = 6
MXU_TILE = 256


def _inproj_kernel(x_ref, g_ref, w_ref, hg_ref, p_ref, o_ref):
    xn = _rms_rows(x_ref[...], g_ref[...]).astype(BF16)
    for s in range(N_SEC):
        y = _dot(xn, w_ref[:, s * SEC:(s + 1) * SEC])
        if s % 3 == 2:
            o_ref[:, s * SEC:(s + 1) * SEC] = y.astype(BF16)
        else:
            y2 = (y * y).astype(BF16)
            ms = jnp.concatenate(
                [_dot(y2[:, c:c + MXU_TILE], p_ref[...]) for c in range(0, SEC, MXU_TILE)], axis=1)
            o_ref[:, s * SEC:(s + 1) * SEC] = (
                y * lax.rsqrt(ms + NORM_EPS) * hg_ref[s:s + 1, :]).astype(BF16)


def _attn_inproj(x2, g, w, head_gains, tm=512):
    T, D = x2.shape
    N = w.shape[1]
    blk = np.kron(np.eye(MXU_TILE // HEAD_DIM), np.full((HEAD_DIM, HEAD_DIM), 1.0 / HEAD_DIM))
    pmat = jnp.asarray(blk, dtype=BF16)
    return pl.pallas_call(
        _inproj_kernel,
        grid=(T // tm,),
        in_specs=[
            pl.BlockSpec((tm, D), lambda i: (i, 0)),
            pl.BlockSpec((1, D), lambda i: (0, 0)),
            pl.BlockSpec((D, N), lambda i: (0, 0)),
            pl.BlockSpec((N_SEC, SEC), lambda i: (0, 0)),
            pl.BlockSpec((MXU_TILE, MXU_TILE), lambda i: (0, 0)),
        ],
        out_specs=pl.BlockSpec((tm, N), lambda i: (i, 0)),
        out_shape=jax.ShapeDtypeStruct((T, N), BF16),
        compiler_params=_params(("parallel",)),
        name="attn_inproj",
    )(x2, g, w, head_gains, pmat)


MIX_ROWS = 512
BLOCK_UNROLL = 16
ROW_GROUP = 16
ROW_PITCH = 24


def _spread_row(t):
    return (t // ROW_GROUP) * ROW_PITCH + t % ROW_GROUP


def _dilated_kernel(slopes_ref, q_ref, k_ref, v_ref, o_ref, qf, kf, vf, qp, kp, vp, ob, db, mb,
                    sb, bias_buf):
    assert [d % ROW_GROUP == 0 for _, d in DILATED_CONFIGS] == [False, False, True]
    S = q_ref.shape[0]
    pad = kf.shape[0] - S
    g = pl.program_id(1)
    qf[...] = q_ref[...].astype(F32)
    kf[0:pad, :] = jnp.zeros((pad, LANES), F32)
    vf[0:pad, :] = jnp.zeros((pad, LANES), F32)
    kf[pad:, :] = k_ref[...].astype(F32)
    vf[pad:, :] = v_ref[...].astype(F32)
    kp[0:_spread_row(pad), :] = jnp.zeros((_spread_row(pad), LANES), F32)
    vp[0:_spread_row(pad), :] = jnp.zeros((_spread_row(pad), LANES), F32)

    def spread(grp, carry):
        src = pl.ds(pl.multiple_of(grp * ROW_GROUP, ROW_GROUP), ROW_GROUP)
        dst = pl.ds(pl.multiple_of(grp * ROW_PITCH, SUBLANES), ROW_GROUP)
        dst_kv = pl.ds(pl.multiple_of(grp * ROW_PITCH + _spread_row(pad), SUBLANES), ROW_GROUP)
        qp[dst, :] = q_ref[src, :].astype(F32)
        kp[dst_kv, :] = k_ref[src, :].astype(F32)
        vp[dst_kv, :] = v_ref[src, :].astype(F32)
        return carry
    lax.fori_loop(0, S // ROW_GROUP, spread, 0, unroll=8)

    lo = lax.broadcasted_iota(jnp.int32, (BAND, LANES), 1) < HEAD_DIM
    ii = lax.broadcasted_iota(jnp.int32, (BAND, 2 * BAND), 0)
    jj = lax.broadcasted_iota(jnp.int32, (BAND, 2 * BAND), 1)
    delta = ii + BAND - jj
    in_band = (delta >= 0) & (delta <= BAND)
    prev_half = lax.broadcasted_iota(jnp.int32, (2 * BAND, 2 * BAND), 1) < BAND
    sl0 = slopes_ref[2 * g]
    sl1 = slopes_ref[2 * g + 1]
    ones = jnp.ones((2 * BAND, LANES), BF16)

    for bi, (window, d) in enumerate(DILATED_CONFIGS):
        assert window == BAND * d
        span = BAND * d
        nbs = S // span
        dist = (delta * d).astype(F32)
        bias = jnp.concatenate(
            [jnp.where(in_band, -sl0 * dist, NEG_INF),
             jnp.where(in_band, -sl1 * dist, NEG_INF)], axis=0)
        bias_buf[2 * bi] = bias
        bias_buf[2 * bi + 1] = jnp.where(prev_half, NEG_INF, bias)

        def body(blk, carry, d=d, span=span, nbs=nbs, bi=bi):
            r = blk // nbs
            nb = blk - r * nbs
            start = nb * span + r
            if d % ROW_GROUP == 0:
                stride = d // ROW_GROUP * ROW_PITCH
                q_row = nb * _spread_row(span) + r
                kv_row = q_row + _spread_row(pad) - _spread_row(span)
                q = qp[pl.ds(q_row, BAND, stride=stride), :].astype(BF16)
                k = kp[pl.ds(kv_row, 2 * BAND, stride=stride), :].astype(BF16)
                v = vp[pl.ds(kv_row, 2 * BAND, stride=stride), :].astype(BF16)
            else:
                q = qf[pl.ds(start, BAND, stride=d), :].astype(BF16)
                k = kf[pl.ds(pad + start - span, 2 * BAND, stride=d), :].astype(BF16)
                v = vf[pl.ds(pad + start - span, 2 * BAND, stride=d), :].astype(BF16)
            zero = jnp.zeros_like(q)
            q2 = jnp.concatenate([jnp.where(lo, q, zero), jnp.where(lo, zero, q)], axis=0)
            s = _dot_nt(q2, k) + bias_buf[2 * bi + jnp.where(nb == 0, 1, 0)]
            m = jnp.max(s, axis=1, keepdims=True)
            e = jnp.exp2(s - m)
            pv = _dot(e.astype(BF16), jnp.concatenate([v, ones], axis=1))
            num = jnp.where(lo, pv[:BAND, :LANES], pv[BAND:, :LANES])
            den = jnp.where(lo, pv[:BAND, LANES:], pv[BAND:, LANES:])
            top = jnp.where(lo, jnp.broadcast_to(m[:BAND], (BAND, LANES)),
                            jnp.broadcast_to(m[BAND:], (BAND, LANES)))
            if d % ROW_GROUP == 0:
                rows = pl.ds(q_row, BAND, stride=stride)
                sb[0, rows, :] = num
                sb[1, rows, :] = den
                sb[2, rows, :] = top
            else:
                rows = pl.ds(start, BAND, stride=d)
                ob[bi, rows, :] = num
                db[bi, rows, :] = den
                mb[bi, rows, :] = top
            return carry

        lax.fori_loop(0, S // BAND, body, 0, unroll=BLOCK_UNROLL)

    rows = MIX_ROWS
    def mix(c, carry):
        sl = pl.ds(pl.multiple_of(c * rows, rows), rows)
        base = pl.multiple_of(c * _spread_row(rows), SUBLANES)

        def spread_chunk(t):
            return jnp.concatenate(
                [sb[t, pl.ds(base + gi * ROW_PITCH, ROW_GROUP), :] for gi in range(rows // ROW_GROUP)],
                axis=0)
        m0, m1, m2 = mb[0, sl, :], mb[1, sl, :], spread_chunk(2)
        m = jnp.maximum(jnp.maximum(m0, m1), m2)
        w0, w1, w2 = jnp.exp2(m0 - m), jnp.exp2(m1 - m), jnp.exp2(m2 - m)
        num = w0 * ob[0, sl, :] + w1 * ob[1, sl, :] + w2 * spread_chunk(0)
        den = w0 * db[0, sl, :] + w1 * db[1, sl, :] + w2 * spread_chunk(1)
        o_ref[sl, :] = (num / den).astype(o_ref.dtype)
        return carry
    lax.fori_loop(0, S // rows, mix, 0)


def _dilated_attention(proj3, slopes):
    B, S, _ = proj3.shape
    assert S % MAX_WINDOW == 0
    pairs = A_HEADS // 2
    blk = lambda off: pl.BlockSpec((None, S, LANES), lambda b, g, off=off: (b, 0, off + g))
    return pl.pallas_call(
        _dilated_kernel,
        grid=(B, pairs),
        in_specs=[
            pl.BlockSpec(memory_space=pltpu.SMEM),
            blk(0), blk(pairs), blk(2 * pairs),
        ],
        out_specs=pl.BlockSpec((None, S, LANES), lambda b, g: (b, 0, g)),
        out_shape=jax.ShapeDtypeStruct((B, S, A_HEADS * HEAD_DIM), BF16),
        scratch_shapes=[
            pltpu.VMEM((S, LANES), F32),
            pltpu.VMEM((MAX_WINDOW + S, LANES), F32),
            pltpu.VMEM((MAX_WINDOW + S, LANES), F32),
            pltpu.VMEM((_spread_row(S), LANES), F32),
            pltpu.VMEM((_spread_row(MAX_WINDOW + S), LANES), F32),
            pltpu.VMEM((_spread_row(MAX_WINDOW + S), LANES), F32),
            pltpu.VMEM((len(DILATED_CONFIGS) - 1, S, LANES), F32),
            pltpu.VMEM((len(DILATED_CONFIGS) - 1, S, LANES), F32),
            pltpu.VMEM((len(DILATED_CONFIGS) - 1, S, LANES), F32),
            pltpu.VMEM((3, _spread_row(S), LANES), F32),
            pltpu.VMEM((2 * len(DILATED_CONFIGS), 2 * BAND, 2 * BAND), F32),
        ],
        compiler_params=_params(("parallel", "parallel")),
        name="dilated_attn",
    )(slopes, proj3, proj3, proj3)


TQ = 512
KV_UNROLL = 2
DIFF_HEADS = 4


def _diff_kernel(slopes_ref, q_ref, k_ref, v_ref, lam_ref, sg_ref, o_ref,
                 m_ref, acc_ref, *, lam_init):
    hg = pl.program_id(1)
    qi = pl.program_id(2)
    lo = lax.broadcasted_iota(jnp.int32, (TQ, LANES), 1) < HEAD_DIM
    heads = range(DIFF_HEADS)
    cols = [slice(hh * LANES, (hh + 1) * LANES) for hh in heads]
    slopes = [slopes_ref[hg * DIFF_HEADS + hh] for hh in heads]
    q2 = []
    for hh in heads:
        q = q_ref[:, cols[hh]]
        zero = jnp.zeros_like(q)
        q2.append(jnp.concatenate([jnp.where(lo, q, zero), jnp.where(lo, zero, q)], axis=0))

    m_ref[...] = jnp.full(m_ref.shape, NEG_INF, F32)
    acc_ref[...] = jnp.zeros(acc_ref.shape, F32)

    def update(hh, key0, nk, row_blocks, masked):
        ks = pl.ds(pl.multiple_of(key0, nk), nk)
        k = k_ref[ks, cols[hh]]
        v1 = jnp.concatenate([v_ref[ks, cols[hh]], jnp.ones((nk, LANES), BF16)], axis=1)
        qs = jnp.concatenate([q2[hh][r0:r0 + rn] for r0, rn in row_blocks], axis=0)
        nrows = qs.shape[0]
        kpos = lax.broadcasted_iota(jnp.int32, (1, nk), 1).astype(F32) + key0.astype(F32)
        s = _dot_nt(qs, k) + slopes[hh] * kpos
        if masked:
            rn = row_blocks[0][1]
            rel = (lax.broadcasted_iota(jnp.int32, (nrows, nk), 0) % rn
                   - lax.broadcasted_iota(jnp.int32, (nrows, nk), 1))
            s = jnp.where(rel >= 0, s, NEG_INF)
        m_old = jnp.concatenate([m_ref[hh, r0:r0 + rn] for r0, rn in row_blocks], axis=0)
        a_old = jnp.concatenate([acc_ref[hh, r0:r0 + rn] for r0, rn in row_blocks], axis=0)
        m_new = jnp.maximum(m_old, jnp.max(s, axis=1, keepdims=True))
        alpha = jnp.exp2(m_old - m_new)
        e = jnp.exp2(s - jnp.concatenate([m_new] * (nk // LANES), axis=1))
        a_new = jnp.concatenate([alpha, alpha], axis=1) * a_old + _dot(e.astype(BF16), v1)
        at = 0
        for r0, rn in row_blocks:
            m_ref[hh, r0:r0 + rn] = m_new[at:at + rn]
            acc_ref[hh, r0:r0 + rn] = a_new[at:at + rn]
            at += rn

    half = TQ // 2

    def step(j, masked, hh):
        if not masked:
            update(hh, j * TQ, TQ, [(0, 2 * TQ)], False)
            return
        update(hh, j * TQ, half, [(0, TQ), (TQ, TQ)], True)
        update(hh, j * TQ + half, half, [(half, half), (TQ + half, half)], True)

    def body(jq, carry):
        for u in range(KV_UNROLL):
            for hh in heads:
                step(KV_UNROLL * jq + u, False, hh)
        return carry
    nq = qi // KV_UNROLL
    lax.fori_loop(0, nq, body, 0)

    rem = qi - nq * KV_UNROLL
    for r in range(KV_UNROLL):
        @pl.when(rem == r)
        def _(r=r):
            for u in range(r):
                for hh in heads:
                    step(qi - r + u, False, hh)
            for hh in heads:
                step(qi, True, hh)

    lq = lam_ref[...]
    lam = (jnp.exp(jnp.sum(lq[0:1] * lq[1:2], axis=1, keepdims=True))
           - jnp.exp(jnp.sum(lq[2:3] * lq[3:4], axis=1, keepdims=True)) + lam_init)
    for hh in heads:
        on = acc_ref[hh, :, 0:LANES] / acc_ref[hh, :, LANES:]
        o = on[:TQ] - lam * on[TQ:]
        o = _rms_rows(o, sg_ref[...]) * (1.0 - lam_init)
        o_ref[:, cols[hh]] = o.astype(o_ref.dtype)


def _diff_attention(proj3, slopes, lam_vecs, sub_gain, lam_init):
    B, S, _ = proj3.shape
    width = DIFF_HEADS * LANES
    groups = B_HEADS // DIFF_HEADS
    qoff = 3 * A_HEADS * HEAD_DIM // width
    koff = qoff + groups
    voff = koff + groups
    return pl.pallas_call(
        functools.partial(_diff_kernel, lam_init=lam_init),
        grid=(B, groups, S // TQ),
        in_specs=[
            pl.BlockSpec(memory_space=pltpu.SMEM),
            pl.BlockSpec((None, TQ, width), lambda b, h, i: (b, i, qoff + h)),
            pl.BlockSpec((None, S, width), lambda b, h, i: (b, 0, koff + h)),
            pl.BlockSpec((None, S, width), lambda b, h, i: (b, 0, voff + h)),
            pl.BlockSpec((4, HEAD_DIM), lambda b, h, i: (0, 0)),
            pl.BlockSpec((1, 2 * HEAD_DIM), lambda b, h, i: (0, 0)),
        ],
        out_specs=pl.BlockSpec((None, TQ, width), lambda b, h, i: (b, i, h)),
        out_shape=jax.ShapeDtypeStruct((B, S, B_HEADS * 2 * HEAD_DIM), BF16),
        scratch_shapes=[
            pltpu.VMEM((DIFF_HEADS, 2 * TQ, LANES), F32),
            pltpu.VMEM((DIFF_HEADS, 2 * TQ, 2 * LANES), F32),
        ],
        compiler_params=_params(("parallel", "parallel", "parallel")),
        name="diff_attn",
    )(slopes, proj3, proj3, proj3, lam_vecs, sub_gain)


FFN_CHUNK = 512


def _seg_pitch(seg):
    p = seg // SUBLANES + 1
    return SUBLANES * (p if p % 2 else p + 1)


def _to_segment_rows(src, stage, dst_ref):
    tm, C = src.shape
    seg = tm // SUBLANES
    pitch = _seg_pitch(seg)
    for n in range(C // LANES):
        cols = slice(n * LANES, (n + 1) * LANES)
        for s in range(SUBLANES):
            stage[n, s * pitch:s * pitch + seg, :] = src[s * seg:(s + 1) * seg, cols]
    for j in range(seg):
        dst_ref[j * SUBLANES:(j + 1) * SUBLANES, :] = jnp.concatenate(
            [stage[n, pl.ds(j, SUBLANES, stride=pitch), :] for n in range(C // LANES)], axis=1)


def _from_segment_rows(val, stage, dst_ref):
    tm, C = val.shape
    seg = tm // SUBLANES
    pitch = _seg_pitch(seg)
    for n in range(C // LANES):
        cols = slice(n * LANES, (n + 1) * LANES)
        for j in range(seg):
            stage[n, pl.ds(j, SUBLANES, stride=pitch), :] = val[j * SUBLANES:(j + 1) * SUBLANES, cols]
        for s in range(SUBLANES):
            dst_ref[s * seg:(s + 1) * seg, cols] = stage[n, s * pitch:s * pitch + seg, :]


def _segment_rows_ref(src, seg_in, stage, hseg):
    if seg_in:
        return src
    _to_segment_rows(src, stage, hseg)
    return hseg


def _store_segment_rows(val, seg_out, stage, o_ref):
    if seg_out:
        o_ref[...] = val
    else:
        _from_segment_rows(val, stage, o_ref)


def _segment_conv(u, carry_ref, cols, w_ref, b_ref, taps):
    tm = u.shape[0]
    first_sublane = lax.broadcasted_iota(jnp.int32, (SUBLANES, u.shape[1]), 0) == 0
    wrapped = []
    for i in range(1, taps):
        tail = u[tm - i * SUBLANES:tm - (i - 1) * SUBLANES]
        wrapped.append(jnp.where(first_sublane,
                                 pltpu.roll(carry_ref[i - 1, :, cols], 1, 0),
                                 pltpu.roll(tail, 1, 0)))
        carry_ref[i - 1, :, cols] = tail
    out = b_ref[:, cols] + w_ref[taps - 1:taps, cols] * u
    for k in range(1, taps):
        shifted = jnp.concatenate(wrapped[k - 1::-1] + [u[:tm - k * SUBLANES]], axis=0)
        out = out + w_ref[taps - 1 - k:taps - k, cols] * shifted
    return out


def _ffn_kernel(*refs, taps, with_attn, seg_in, seg_out):
    if with_attn:
        x_ref, a_ref, b_ref, wo_ref, *refs = refs
        na = a_ref.shape[1]
        h = x_ref[...] + _dot(a_ref[...], wo_ref[0:na, :]) + _dot(b_ref[...], wo_ref[na:, :])
    else:
        h, *refs = refs
    g_ref, wup_ref, cw_ref, cb_ref, wdn_ref, o_ref, stage, hseg, carry, act = refs
    F = wdn_ref.shape[0]

    @pl.when(pl.program_id(1) == 0)
    def _():
        carry[...] = jnp.zeros(carry.shape, F32)

    hs_ref = _segment_rows_ref(h, seg_in, stage, hseg)
    xn = _rms_rows(hs_ref[...], g_ref[...]).astype(BF16)
    for c in range(F // FFN_CHUNK):
        gc = slice(c * FFN_CHUNK, (c + 1) * FFN_CHUNK)
        vc = slice(F + c * FFN_CHUNK, F + (c + 1) * FFN_CHUNK)
        gg = _segment_conv(_dot(xn, wup_ref[:, gc]), carry, gc, cw_ref, cb_ref, taps)
        vv = _segment_conv(_dot(xn, wup_ref[:, vc]), carry, vc, cw_ref, cb_ref, taps)
        act[:, gc] = (_gelu(gg) * vv).astype(BF16)
    _store_segment_rows(hs_ref[...] + _dot(act[...], wdn_ref[...]), seg_out, stage, o_ref)


def _stage_shape(tm, C):
    return (C // LANES, SUBLANES * _seg_pitch(tm // SUBLANES), LANES)


def _resident(shape):
    return pl.BlockSpec(shape, lambda b, i: (0,) * len(shape), pipeline_mode=pl.Buffered(1))


SEQ_TILE = 512


def _conv_ffn(h3, g, wup, cw, cb, wdn, attn=None, seg_in=False, seg_out=False, tm=SEQ_TILE):
    assert not (seg_in and attn is not None)
    B, S, D = h3.shape
    F2 = wup.shape[1]
    taps = cw.shape[0]
    row_tile = lambda width: pl.BlockSpec((None, tm, width), lambda b, i: (b, i, 0))
    attn_args, attn_specs = (), []
    if attn is not None:
        oa, ob, wo = attn
        attn_args = (oa, ob, wo)
        attn_specs = [row_tile(oa.shape[2]), row_tile(ob.shape[2]), _resident(wo.shape)]
    return pl.pallas_call(
        functools.partial(_ffn_kernel, taps=taps, with_attn=attn is not None,
                          seg_in=seg_in, seg_out=seg_out),
        grid=(B, S // tm),
        in_specs=[
            row_tile(D),
            *attn_specs,
            _resident((1, D)),
            _resident(wup.shape),
            _resident(cw.shape),
            _resident((1, F2)),
            _resident(wdn.shape),
        ],
        out_specs=row_tile(D),
        out_shape=jax.ShapeDtypeStruct((B, S, D), F32),
        scratch_shapes=[
            pltpu.VMEM(_stage_shape(tm, D), F32),
            pltpu.VMEM((tm, D), F32),
            pltpu.VMEM((taps - 1, SUBLANES, F2), F32),
            pltpu.VMEM((tm, F2 // 2), BF16),
        ],
        compiler_params=_params(("arbitrary", "arbitrary")),
        name="conv_ffn",
    )(h3, *attn_args, g, wup, cw, cb, wdn)


def _rec_kernel(h_ref, g_ref, win_ref, cw_ref, cb_ref, wa_ref, ba_ref, wx_ref, bx_ref,
                ap_ref, wout_ref, o_ref, stage, hseg, carry, hstate, *, taps, seg_in, seg_out):
    tm, C = h_ref.shape
    seg = tm // SUBLANES
    bw = C // LRU_BLOCKS

    @pl.when(pl.program_id(1) == 0)
    def _():
        carry[...] = jnp.zeros(carry.shape, F32)
        hstate[...] = jnp.zeros(hstate.shape, F32)

    hs_ref = _segment_rows_ref(h_ref, seg_in, stage, hseg)
    xn = _rms_rows(hs_ref[...], g_ref[...]).astype(BF16)
    gate = _dot(xn, win_ref[:, 0:C])
    xr_all = _segment_conv(_dot(xn, win_ref[:, C:]), carry, slice(0, C), cw_ref, cb_ref, taps)

    ap = ap_ref[...]
    decay = -LRU_C * (jnp.maximum(-ap, 0.0) + jnp.log1p(jnp.exp(-jnp.abs(ap))))
    sublane = lax.broadcasted_iota(jnp.int32, (SUBLANES, bw), 0)
    blocks = []
    for n in range(LRU_BLOCKS):
        cols = slice(n * bw, (n + 1) * bw)
        xr = xr_all[:, cols]
        xb = xr.astype(BF16)
        r = _sigmoid(_dot(xb, wa_ref[n]) + ba_ref[:, cols])
        i = _sigmoid(_dot(xb, wx_ref[n]) + bx_ref[:, cols])
        log_a = decay[:, cols] * r
        a = jnp.exp(log_a)
        u = jnp.sqrt(-jnp.tanh(log_a) * (1.0 + a * a)) * (i * xr)

        hl = jnp.zeros((SUBLANES, bw), F32)
        pp = jnp.ones((SUBLANES, bw), F32)
        for j in range(seg):
            aj = a[j * SUBLANES:(j + 1) * SUBLANES]
            hl = aj * hl + u[j * SUBLANES:(j + 1) * SUBLANES]
            pp = aj * pp
        cin = hstate[0:1, cols]
        h_in = jnp.zeros((SUBLANES, bw), F32)
        for s in range(SUBLANES):
            h_in = jnp.where(sublane == s, cin, h_in)
            cin = hl[s:s + 1] + pp[s:s + 1] * cin
        hstate[:, cols] = jnp.broadcast_to(cin, (SUBLANES, bw))
        rows = []
        for j in range(seg):
            h_in = a[j * SUBLANES:(j + 1) * SUBLANES] * h_in + u[j * SUBLANES:(j + 1) * SUBLANES]
            rows.append(h_in)
        blocks.append(jnp.concatenate(rows, axis=0))

    hs = jnp.concatenate(blocks, axis=1)
    y = (hs * _gelu(gate)).astype(BF16)
    _store_segment_rows(hs_ref[...] + _dot(y, wout_ref[...]), seg_out, stage, o_ref)


def _recurrent_block(h3, g, win, cw, cb, wa, ba, wx, bx, ap, wout,
                     seg_in=False, seg_out=False, tm=SEQ_TILE):
    B, S, D = h3.shape
    C = wout.shape[0]
    taps = cw.shape[0]
    unused = (SUBLANES, LANES)
    stage_shape = unused if (seg_in and seg_out) else _stage_shape(tm, D)
    hseg_shape = unused if seg_in else (tm, D)
    return pl.pallas_call(
        functools.partial(_rec_kernel, taps=taps, seg_in=seg_in, seg_out=seg_out),
        grid=(B, S // tm),
        in_specs=[
            pl.BlockSpec((None, tm, D), lambda b, i: (b, i, 0)),
            _resident((1, D)), _resident(win.shape), _resident(cw.shape), _resident((1, C)),
            _resident(wa.shape), _resident((1, C)), _resident(wx.shape), _resident((1, C)),
            _resident((1, C)), _resident(wout.shape),
        ],
        out_specs=pl.BlockSpec((None, tm, D), lambda b, i: (b, i, 0)),
        out_shape=jax.ShapeDtypeStruct((B, S, D), F32),
        scratch_shapes=[
            pltpu.VMEM(stage_shape, F32),
            pltpu.VMEM(hseg_shape, F32),
            pltpu.VMEM((taps - 1, SUBLANES, C), F32),
            pltpu.VMEM((SUBLANES, C), F32),
        ],
        compiler_params=_params(("arbitrary", "arbitrary")),
        name="recurrent_block",
    )(h3, g, win, cw, cb, wa, ba, wx, bx, ap, wout)


def _alibi_slopes(n):
    return jnp.exp2(-8.0 * jnp.arange(1, n + 1, dtype=F32) / n)


def _row(v):
    return v.reshape(1, -1).astype(F32)


def kernel(x, attn_norm, attn_w_in, attn_w_out, a_q_norm, a_k_norm, b_q_norm, b_k_norm, b_sub_norm,
           b_lam_q1, b_lam_k1, b_lam_q2, b_lam_k2, rec_norm, rec_w_in, rec_conv_w, rec_conv_b,
           rec_gate_a_w, rec_gate_a_b, rec_gate_x_w, rec_gate_x_b, rec_a_param, rec_w_out,
           ffn_norm, ffn_w_up, ffn_conv_w, ffn_conv_b, ffn_w_down):
    B, S, D = x.shape
    depth = ffn_norm.shape[0]
    slopes = _alibi_slopes(A_HEADS + B_HEADS)
    h = x
    h_is_seg = False
    for layer in range(depth):
        j = layer // 2
        attn = None
        if layer % 2 == 0:
            lam_init = 0.8 - 0.6 * math.exp(-0.3 * layer)
            scale = HEAD_DIM ** -0.5
            reps = SEC // HEAD_DIM
            ones = jnp.ones((SEC,), F32)
            head_gains = jnp.stack([
                jnp.tile(a_q_norm[j].astype(F32), reps) * (scale * LOG2E),
                jnp.tile(a_k_norm[j].astype(F32), reps), ones,
                jnp.tile(b_q_norm[j].astype(F32), reps) * (scale * LOG2E),
                jnp.tile(b_k_norm[j].astype(F32), reps), ones])
            proj = _attn_inproj(h.reshape(B * S, D), _row(attn_norm[j]),
                                attn_w_in[j].astype(BF16), head_gains)
            proj3 = proj.reshape(B, S, -1)
            oa = _dilated_attention(proj3, slopes[:A_HEADS] * LOG2E)
            lam_vecs = jnp.stack([b_lam_q1[j], b_lam_k1[j], b_lam_q2[j], b_lam_k2[j]]).astype(F32)
            ob = _diff_attention(proj3, slopes[A_HEADS:] * LOG2E, lam_vecs, _row(b_sub_norm[j]),
                                 lam_init)
            attn = (oa, ob, attn_w_out[j].astype(BF16))
        else:
            h = _recurrent_block(
                h, _row(rec_norm[j]), rec_w_in[j].astype(BF16), rec_conv_w[j].astype(F32),
                _row(rec_conv_b[j]), rec_gate_a_w[j].astype(BF16), _row(rec_gate_a_b[j]),
                rec_gate_x_w[j].astype(BF16), _row(rec_gate_x_b[j]), _row(rec_a_param[j]),
                rec_w_out[j].astype(BF16), seg_in=h_is_seg, seg_out=True)
            h_is_seg = True
        next_is_rec = layer + 1 < depth and (layer + 1) % 2 == 1
        h = _conv_ffn(h, _row(ffn_norm[layer]), ffn_w_up[layer].astype(BF16),
                      ffn_conv_w[layer].astype(F32), _row(ffn_conv_b[layer]),
                      ffn_w_down[layer].astype(BF16), attn=attn,
                      seg_in=h_is_seg, seg_out=next_is_rec)
        h_is_seg = next_is_rec
    return h
```

```python
import functools
import math

import numpy as np
import jax
import jax.numpy as jnp
from jax import lax
from jax.experimental import pallas as pl
from jax.experimental.pallas import tpu as pltpu

F32 = jnp.float32
BF16 = jnp.bfloat16

HEAD_DIM = 64
A_HEADS = 8
B_HEADS = 4
DILATED_CONFIGS = ((128, 1), (512, 4), (2048, 16))
BAND = 128
MAX_WINDOW = 2048
LRU_BLOCKS = 8
LRU_C = 8.0
NORM_EPS = 1e-6
NEG_INF = -1e30
LOG2E = math.log2(math.e)
LANES = 128
SUBLANES = 8
VMEM_LIMIT = 56 * 1024 * 1024


def _gelu(x):
    c = math.sqrt(2.0 / math.pi)
    return x * (0.5 * (1.0 + jnp.tanh(c * (x + 0.044715 * (x * x * x)))))


def _sigmoid(x):
    return 1.0 / (1.0 + jnp.exp(-x))


def _rms_rows(x, g):
    ms = jnp.mean(x * x, axis=-1, keepdims=True)
    return x * lax.rsqrt(ms + NORM_EPS) * g


def _dot(a, b):
    return jnp.dot(a, b, preferred_element_type=F32)


def _dot_nt(a, b):
    return lax.dot_general(a, b, (((1,), (1,)), ((), ())), preferred_element_type=F32)


def _params(sem):
    return pltpu.CompilerParams(dimension_semantics=sem, vmem_limit_bytes=VMEM_LIMIT)


SEC = 512
N_SEC = 6
MXU_TILE = 256


def _inproj_kernel(x_ref, g_ref, w_ref, hg_ref, p_ref, o_ref):
    xn = _rms_rows(x_ref[...], g_ref[...]).astype(BF16)
    for s in range(N_SEC):
        y = _dot(xn, w_ref[:, s * SEC:(s + 1) * SEC])
        if s % 3 == 2:
            o_ref[:, s * SEC:(s + 1) * SEC] = y.astype(BF16)
        else:
            y2 = (y * y).astype(BF16)
            ms = jnp.concatenate(
                [_dot(y2[:, c:c + MXU_TILE], p_ref[...]) for c in range(0, SEC, MXU_TILE)], axis=1)
            o_ref[:, s * SEC:(s + 1) * SEC] = (
                y * lax.rsqrt(ms + NORM_EPS) * hg_ref[s:s + 1, :]).astype(BF16)


def _attn_inproj(x2, g, w, head_gains, tm=1024):
    T, D = x2.shape
    N = w.shape[1]
    blk = np.kron(np.eye(MXU_TILE // HEAD_DIM), np.full((HEAD_DIM, HEAD_DIM), 1.0 / HEAD_DIM))
    pmat = jnp.asarray(blk, dtype=BF16)
    return pl.pallas_call(
        _inproj_kernel,
        grid=(T // tm,),
        in_specs=[
            pl.BlockSpec((tm, D), lambda i: (i, 0)),
            pl.BlockSpec((1, D), lambda i: (0, 0)),
            pl.BlockSpec((D, N), lambda i: (0, 0)),
            pl.BlockSpec((N_SEC, SEC), lambda i: (0, 0)),
            pl.BlockSpec((MXU_TILE, MXU_TILE), lambda i: (0, 0)),
        ],
        out_specs=pl.BlockSpec((tm, N), lambda i: (i, 0)),
        out_shape=jax.ShapeDtypeStruct((T, N), BF16),
        compiler_params=_params(("parallel",)),
        name="attn_inproj",
    )(x2, g, w, head_gains, pmat)


MIX_ROWS = 512
BLOCK_UNROLL = 16
ROW_GROUP = 16
ROW_PITCH = 24


def _spread_row(t):
    return (t // ROW_GROUP) * ROW_PITCH + t % ROW_GROUP


def _dilated_kernel(slopes_ref, q_ref, k_ref, v_ref, o_ref, qf, kf, vf, qp, kp, vp, ob, db, mb,
                    sb, bias_buf):
    assert [d % ROW_GROUP == 0 for _, d in DILATED_CONFIGS] == [False, False, True]
    S = q_ref.shape[0]
    pad = kf.shape[0] - S
    g = pl.program_id(1)
    qf[...] = q_ref[...].astype(F32)
    kf[0:pad, :] = jnp.zeros((pad, LANES), F32)
    vf[0:pad, :] = jnp.zeros((pad, LANES), F32)
    kf[pad:, :] = k_ref[...].astype(F32)
    vf[pad:, :] = v_ref[...].astype(F32)
    kp[0:_spread_row(pad), :] = jnp.zeros((_spread_row(pad), LANES), F32)
    vp[0:_spread_row(pad), :] = jnp.zeros((_spread_row(pad), LANES), F32)

    def spread(grp, carry):
        src = pl.ds(pl.multiple_of(grp * ROW_GROUP, ROW_GROUP), ROW_GROUP)
        dst = pl.ds(pl.multiple_of(grp * ROW_PITCH, SUBLANES), ROW_GROUP)
        dst_kv = pl.ds(pl.multiple_of(grp * ROW_PITCH + _spread_row(pad), SUBLANES), ROW_GROUP)
        qp[dst, :] = q_ref[src, :].astype(F32)
        kp[dst_kv, :] = k_ref[src, :].astype(F32)
        vp[dst_kv, :] = v_ref[src, :].astype(F32)
        return carry
    lax.fori_loop(0, S // ROW_GROUP, spread, 0, unroll=8)

    lo = lax.broadcasted_iota(jnp.int32, (BAND, LANES), 1) < HEAD_DIM
    ii = lax.broadcasted_iota(jnp.int32, (BAND, 2 * BAND), 0)
    jj = lax.broadcasted_iota(jnp.int32, (BAND, 2 * BAND), 1)
    delta = ii + BAND - jj
    in_band = (delta >= 0) & (delta <= BAND)
    prev_half = lax.broadcasted_iota(jnp.int32, (2 * BAND, 2 * BAND), 1) < BAND
    sl0 = slopes_ref[2 * g]
    sl1 = slopes_ref[2 * g + 1]
    ones = jnp.ones((2 * BAND, LANES), BF16)

    for bi, (window, d) in enumerate(DILATED_CONFIGS):
        assert window == BAND * d
        span = BAND * d
        nbs = S // span
        dist = (delta * d).astype(F32)
        bias = jnp.concatenate(
            [jnp.where(in_band, -sl0 * dist, NEG_INF),
             jnp.where(in_band, -sl1 * dist, NEG_INF)], axis=0)
        bias_buf[2 * bi] = bias
        bias_buf[2 * bi + 1] = jnp.where(prev_half, NEG_INF, bias)

        def body(blk, carry, d=d, span=span, nbs=nbs, bi=bi):
            r = blk // nbs
            nb = blk - r * nbs
            start = nb * span + r
            if d % ROW_GROUP == 0:
                stride = d // ROW_GROUP * ROW_PITCH
                q_row = nb * _spread_row(span) + r
                kv_row = q_row + _spread_row(pad) - _spread_row(span)
                q = qp[pl.ds(q_row, BAND, stride=stride), :].astype(BF16)
                k = kp[pl.ds(kv_row, 2 * BAND, stride=stride), :].astype(BF16)
                v = vp[pl.ds(kv_row, 2 * BAND, stride=stride), :].astype(BF16)
            else:
                q = qf[pl.ds(start, BAND, stride=d), :].astype(BF16)
                k = kf[pl.ds(pad + start - span, 2 * BAND, stride=d), :].astype(BF16)
                v = vf[pl.ds(pad + start - span, 2 * BAND, stride=d), :].astype(BF16)
            zero = jnp.zeros_like(q)
            q2 = jnp.concatenate([jnp.where(lo, q, zero), jnp.where(lo, zero, q)], axis=0)
            s = _dot_nt(q2, k) + bias_buf[2 * bi + jnp.where(nb == 0, 1, 0)]
            m = jnp.max(s, axis=1, keepdims=True)
            e = jnp.exp2(s - m)
            pv = _dot(e.astype(BF16), jnp.concatenate([v, ones], axis=1))
            num = jnp.where(lo, pv[:BAND, :LANES], pv[BAND:, :LANES])
            den = jnp.where(lo, pv[:BAND, LANES:], pv[BAND:, LANES:])
            top = jnp.where(lo, jnp.broadcast_to(m[:BAND], (BAND, LANES)),
                            jnp.broadcast_to(m[BAND:], (BAND, LANES)))
            if d % ROW_GROUP == 0:
                rows = pl.ds(q_row, BAND, stride=stride)
                sb[0, rows, :] = num
                sb[1, rows, :] = den
                sb[2, rows, :] = top
            else:
                rows = pl.ds(start, BAND, stride=d)
                ob[bi, rows, :] = num
                db[bi, rows, :] = den
                mb[bi, rows, :] = top
            return carry

        lax.fori_loop(0, S // BAND, body, 0, unroll=BLOCK_UNROLL)

    rows = MIX_ROWS
    def mix(c, carry):
        sl = pl.ds(pl.multiple_of(c * rows, rows), rows)
        base = pl.multiple_of(c * _spread_row(rows), SUBLANES)

        def spread_chunk(t):
            return jnp.concatenate(
                [sb[t, pl.ds(base + gi * ROW_PITCH, ROW_GROUP), :] for gi in range(rows // ROW_GROUP)],
                axis=0)
        m0, m1, m2 = mb[0, sl, :], mb[1, sl, :], spread_chunk(2)
        m = jnp.maximum(jnp.maximum(m0, m1), m2)
        w0, w1, w2 = jnp.exp2(m0 - m), jnp.exp2(m1 - m), jnp.exp2(m2 - m)
        num = w0 * ob[0, sl, :] + w1 * ob[1, sl, :] + w2 * spread_chunk(0)
        den = w0 * db[0, sl, :] + w1 * db[1, sl, :] + w2 * spread_chunk(1)
        o_ref[sl, :] = (num / den).astype(o_ref.dtype)
        return carry
    lax.fori_loop(0, S // rows, mix, 0)


def _dilated_attention(proj3, slopes):
    B, S, _ = proj3.shape
    assert S % MAX_WINDOW == 0
    pairs = A_HEADS // 2
    blk = lambda off: pl.BlockSpec((None, S, LANES), lambda b, g, off=off: (b, 0, off + g))
    return pl.pallas_call(
        _dilated_kernel,
        grid=(B, pairs),
        in_specs=[
            pl.BlockSpec(memory_space=pltpu.SMEM),
            blk(0), blk(pairs), blk(2 * pairs),
        ],
        out_specs=pl.BlockSpec((None, S, LANES), lambda b, g: (b, 0, g)),
        out_shape=jax.ShapeDtypeStruct((B, S, A_HEADS * HEAD_DIM), BF16),
        scratch_shapes=[
            pltpu.VMEM((S, LANES), F32),
            pltpu.VMEM((MAX_WINDOW + S, LANES), F32),
            pltpu.VMEM((MAX_WINDOW + S, LANES), F32),
            pltpu.VMEM((_spread_row(S), LANES), F32),
            pltpu.VMEM((_spread_row(MAX_WINDOW + S), LANES), F32),
            pltpu.VMEM((_spread_row(MAX_WINDOW + S), LANES), F32),
            pltpu.VMEM((len(DILATED_CONFIGS) - 1, S, LANES), F32),
            pltpu.VMEM((len(DILATED_CONFIGS) - 1, S, LANES), F32),
            pltpu.VMEM((len(DILATED_CONFIGS) - 1, S, LANES), F32),
            pltpu.VMEM((3, _spread_row(S), LANES), F32),
            pltpu.VMEM((2 * len(DILATED_CONFIGS), 2 * BAND, 2 * BAND), F32),
        ],
        compiler_params=_params(("parallel", "parallel")),
        name="dilated_attn",
    )(slopes, proj3, proj3, proj3)


TQ = 512
KV_UNROLL = 2
DIFF_HEADS = 4


def _diff_kernel(slopes_ref, q_ref, k_ref, v_ref, lam_ref, sg_ref, o_ref,
                 m_ref, acc_ref, *, lam_init):
    hg = pl.program_id(1)
    qi = pl.program_id(2)
    lo = lax.broadcasted_iota(jnp.int32, (TQ, LANES), 1) < HEAD_DIM
    heads = range(DIFF_HEADS)
    cols = [slice(hh * LANES, (hh + 1) * LANES) for hh in heads]
    slopes = [slopes_ref[hg * DIFF_HEADS + hh] for hh in heads]
    q2 = []
    for hh in heads:
        q = q_ref[:, cols[hh]]
        zero = jnp.zeros_like(q)
        q2.append(jnp.concatenate([jnp.where(lo, q, zero), jnp.where(lo, zero, q)], axis=0))

    m_ref[...] = jnp.full(m_ref.shape, NEG_INF, F32)
    acc_ref[...] = jnp.zeros(acc_ref.shape, F32)

    def update(hh, key0, nk, row_blocks, masked):
        ks = pl.ds(pl.multiple_of(key0, nk), nk)
        k = k_ref[ks, cols[hh]]
        v1 = jnp.concatenate([v_ref[ks, cols[hh]], jnp.ones((nk, LANES), BF16)], axis=1)
        qs = jnp.concatenate([q2[hh][r0:r0 + rn] for r0, rn in row_blocks], axis=0)
        nrows = qs.shape[0]
        kpos = lax.broadcasted_iota(jnp.int32, (1, nk), 1).astype(F32) + key0.astype(F32)
        s = _dot_nt(qs, k) + slopes[hh] * kpos
        if masked:
            rn = row_blocks[0][1]
            rel = (lax.broadcasted_iota(jnp.int32, (nrows, nk), 0) % rn
                   - lax.broadcasted_iota(jnp.int32, (nrows, nk), 1))
            s = jnp.where(rel >= 0, s, NEG_INF)
        m_old = jnp.concatenate([m_ref[hh, r0:r0 + rn] for r0, rn in row_blocks], axis=0)
        a_old = jnp.concatenate([acc_ref[hh, r0:r0 + rn] for r0, rn in row_blocks], axis=0)
        m_new = jnp.maximum(m_old, jnp.max(s, axis=1, keepdims=True))
        alpha = jnp.exp2(m_old - m_new)
        e = jnp.exp2(s - jnp.concatenate([m_new] * (nk // LANES), axis=1))
        a_new = jnp.concatenate([alpha, alpha], axis=1) * a_old + _dot(e.astype(BF16), v1)
        at = 0
        for r0, rn in row_blocks:
            m_ref[hh, r0:r0 + rn] = m_new[at:at + rn]
            acc_ref[hh, r0:r0 + rn] = a_new[at:at + rn]
            at += rn

    half = TQ // 2

    def step(j, masked, hh):
        if not masked:
            update(hh, j * TQ, TQ, [(0, 2 * TQ)], False)
            return
        update(hh, j * TQ, half, [(0, TQ), (TQ, TQ)], True)
        update(hh, j * TQ + half, half, [(half, half), (TQ + half, half)], True)

    def body(jq, carry):
        for u in range(KV_UNROLL):
            for hh in heads:
                step(KV_UNROLL * jq + u, False, hh)
        return carry
    nq = qi // KV_UNROLL
    lax.fori_loop(0, nq, body, 0)

    rem = qi - nq * KV_UNROLL
    for r in range(KV_UNROLL):
        @pl.when(rem == r)
        def _(r=r):
            for u in range(r):
                for hh in heads:
                    step(qi - r + u, False, hh)
            for hh in heads:
                step(qi, True, hh)

    lq = lam_ref[...]
    lam = (jnp.exp(jnp.sum(lq[0:1] * lq[1:2], axis=1, keepdims=True))
           - jnp.exp(jnp.sum(lq[2:3] * lq[3:4], axis=1, keepdims=True)) + lam_init)
    for hh in heads:
        on = acc_ref[hh, :, 0:LANES] / acc_ref[hh, :, LANES:]
        o = on[:TQ] - lam * on[TQ:]
        o = _rms_rows(o, sg_ref[...]) * (1.0 - lam_init)
        o_ref[:, cols[hh]] = o.astype(o_ref.dtype)


def _diff_attention(proj3, slopes, lam_vecs, sub_gain, lam_init):
    B, S, _ = proj3.shape
    width = DIFF_HEADS * LANES
    groups = B_HEADS // DIFF_HEADS
    qoff = 3 * A_HEADS * HEAD_DIM // width
    koff = qoff + groups
    voff = koff + groups
    return pl.pallas_call(
        functools.partial(_diff_kernel, lam_init=lam_init),
        grid=(B, groups, S // TQ),
        in_specs=[
            pl.BlockSpec(memory_space=pltpu.SMEM),
            pl.BlockSpec((None, TQ, width), lambda b, h, i: (b, i, qoff + h)),
            pl.BlockSpec((None, S, width), lambda b, h, i: (b, 0, koff + h)),
            pl.BlockSpec((None, S, width), lambda b, h, i: (b, 0, voff + h)),
            pl.BlockSpec((4, HEAD_DIM), lambda b, h, i: (0, 0)),
            pl.BlockSpec((1, 2 * HEAD_DIM), lambda b, h, i: (0, 0)),
        ],
        out_specs=pl.BlockSpec((None, TQ, width), lambda b, h, i: (b, i, h)),
        out_shape=jax.ShapeDtypeStruct((B, S, B_HEADS * 2 * HEAD_DIM), BF16),
        scratch_shapes=[
            pltpu.VMEM((DIFF_HEADS, 2 * TQ, LANES), F32),
            pltpu.VMEM((DIFF_HEADS, 2 * TQ, 2 * LANES), F32),
        ],
        compiler_params=_params(("parallel", "parallel", "parallel")),
        name="diff_attn",
    )(slopes, proj3, proj3, proj3, lam_vecs, sub_gain)


FFN_CHUNK = 512


def _seg_pitch(seg):
    p = seg // SUBLANES + 1
    return SUBLANES * (p if p % 2 else p + 1)


def _to_segment_rows(src, stage, dst_ref):
    tm, C = src.shape
    seg = tm // SUBLANES
    pitch = _seg_pitch(seg)
    for n in range(C // LANES):
        cols = slice(n * LANES, (n + 1) * LANES)
        for s in range(SUBLANES):
            stage[n, s * pitch:s * pitch + seg, :] = src[s * seg:(s + 1) * seg, cols]
    for j in range(seg):
        dst_ref[j * SUBLANES:(j + 1) * SUBLANES, :] = jnp.concatenate(
            [stage[n, pl.ds(j, SUBLANES, stride=pitch), :] for n in range(C // LANES)], axis=1)


def _from_segment_rows(val, stage, dst_ref):
    tm, C = val.shape
    seg = tm // SUBLANES
    pitch = _seg_pitch(seg)
    for n in range(C // LANES):
        cols = slice(n * LANES, (n + 1) * LANES)
        for j in range(seg):
            stage[n, pl.ds(j, SUBLANES, stride=pitch), :] = val[j * SUBLANES:(j + 1) * SUBLANES, cols]
        for s in range(SUBLANES):
            dst_ref[s * seg:(s + 1) * seg, cols] = stage[n, s * pitch:s * pitch + seg, :]


def _segment_rows_ref(src, seg_in, stage, hseg):
    if seg_in:
        return src
    _to_segment_rows(src, stage, hseg)
    return hseg


def _store_segment_rows(val, seg_out, stage, o_ref):
    if seg_out:
        o_ref[...] = val
    else:
        _from_segment_rows(val, stage, o_ref)


def _segment_conv(u, carry_ref, cols, w_ref, b_ref, taps):
    tm = u.shape[0]
    first_sublane = lax.broadcasted_iota(jnp.int32, (SUBLANES, u.shape[1]), 0) == 0
    wrapped = []
    for i in range(1, taps):
        tail = u[tm - i * SUBLANES:tm - (i - 1) * SUBLANES]
        wrapped.append(jnp.where(first_sublane,
                                 pltpu.roll(carry_ref[i - 1, :, cols], 1, 0),
                                 pltpu.roll(tail, 1, 0)))
        carry_ref[i - 1, :, cols] = tail
    out = b_ref[:, cols] + w_ref[taps - 1:taps, cols] * u
    for k in range(1, taps):
        shifted = jnp.concatenate(wrapped[k - 1::-1] + [u[:tm - k * SUBLANES]], axis=0)
        out = out + w_ref[taps - 1 - k:taps - k, cols] * shifted
    return out


def _ffn_kernel(*refs, taps, with_attn, seg_in, seg_out):
    if with_attn:
        x_ref, a_ref, b_ref, wo_ref, *refs = refs
        na = a_ref.shape[1]
        h = x_ref[...] + _dot(a_ref[...], wo_ref[0:na, :]) + _dot(b_ref[...], wo_ref[na:, :])
    else:
        h, *refs = refs
    g_ref, wup_ref, cw_ref, cb_ref, wdn_ref, o_ref, stage, hseg, carry, act = refs
    F = wdn_ref.shape[0]

    @pl.when(pl.program_id(1) == 0)
    def _():
        carry[...] = jnp.zeros(carry.shape, F32)

    hs_ref = _segment_rows_ref(h, seg_in, stage, hseg)
    xn = _rms_rows(hs_ref[...], g_ref[...]).astype(BF16)
    for c in range(F // FFN_CHUNK):
        gc = slice(c * FFN_CHUNK, (c + 1) * FFN_CHUNK)
        vc = slice(F + c * FFN_CHUNK, F + (c + 1) * FFN_CHUNK)
        gg = _segment_conv(_dot(xn, wup_ref[:, gc]), carry, gc, cw_ref, cb_ref, taps)
        vv = _segment_conv(_dot(xn, wup_ref[:, vc]), carry, vc, cw_ref, cb_ref, taps)
        act[:, gc] = (_gelu(gg) * vv).astype(BF16)
    _store_segment_rows(hs_ref[...] + _dot(act[...], wdn_ref[...]), seg_out, stage, o_ref)


def _stage_shape(tm, C):
    return (C // LANES, SUBLANES * _seg_pitch(tm // SUBLANES), LANES)


def _resident(shape):
    return pl.BlockSpec(shape, lambda b, i: (0,) * len(shape), pipeline_mode=pl.Buffered(1))


SEQ_TILE = 512


def _conv_ffn(h3, g, wup, cw, cb, wdn, attn=None, seg_in=False, seg_out=False, tm=SEQ_TILE):
    assert not (seg_in and attn is not None)
    B, S, D = h3.shape
    F2 = wup.shape[1]
    taps = cw.shape[0]
    row_tile = lambda width: pl.BlockSpec((None, tm, width), lambda b, i: (b, i, 0))
    attn_args, attn_specs = (), []
    if attn is not None:
        oa, ob, wo = attn
        attn_args = (oa, ob, wo)
        attn_specs = [row_tile(oa.shape[2]), row_tile(ob.shape[2]), _resident(wo.shape)]
    return pl.pallas_call(
        functools.partial(_ffn_kernel, taps=taps, with_attn=attn is not None,
                          seg_in=seg_in, seg_out=seg_out),
        grid=(B, S // tm),
        in_specs=[
            row_tile(D),
            *attn_specs,
            _resident((1, D)),
            _resident(wup.shape),
            _resident(cw.shape),
            _resident((1, F2)),
            _resident(wdn.shape),
        ],
        out_specs=row_tile(D),
        out_shape=jax.ShapeDtypeStruct((B, S, D), F32),
        scratch_shapes=[
            pltpu.VMEM(_stage_shape(tm, D), F32),
            pltpu.VMEM((tm, D), F32),
            pltpu.VMEM((taps - 1, SUBLANES, F2), F32),
            pltpu.VMEM((tm, F2 // 2), BF16),
        ],
        compiler_params=_params(("arbitrary", "arbitrary")),
        name="conv_ffn",
    )(h3, *attn_args, g, wup, cw, cb, wdn)


def _rec_kernel(h_ref, g_ref, win_ref, cw_ref, cb_ref, wa_ref, ba_ref, wx_ref, bx_ref,
                ap_ref, wout_ref, o_ref, stage, hseg, carry, hstate, *, taps, seg_in, seg_out):
    tm, C = h_ref.shape
    seg = tm // SUBLANES
    bw = C // LRU_BLOCKS

    @pl.when(pl.program_id(1) == 0)
    def _():
        carry[...] = jnp.zeros(carry.shape, F32)
        hstate[...] = jnp.zeros(hstate.shape, F32)

    hs_ref = _segment_rows_ref(h_ref, seg_in, stage, hseg)
    xn = _rms_rows(hs_ref[...], g_ref[...]).astype(BF16)
    gate = _dot(xn, win_ref[:, 0:C])
    xr_all = _segment_conv(_dot(xn, win_ref[:, C:]), carry, slice(0, C), cw_ref, cb_ref, taps)

    ap = ap_ref[...]
    decay = -LRU_C * (jnp.maximum(-ap, 0.0) + jnp.log1p(jnp.exp(-jnp.abs(ap))))
    sublane = lax.broadcasted_iota(jnp.int32, (SUBLANES, bw), 0)
    blocks = []
    for n in range(LRU_BLOCKS):
        cols = slice(n * bw, (n + 1) * bw)
        xr = xr_all[:, cols]
        xb = xr.astype(BF16)
        r = _sigmoid(_dot(xb, wa_ref[n]) + ba_ref[:, cols])
        i = _sigmoid(_dot(xb, wx_ref[n]) + bx_ref[:, cols])
        log_a = decay[:, cols] * r
        a = jnp.exp(log_a)
        u = jnp.sqrt(-jnp.tanh(log_a) * (1.0 + a * a)) * (i * xr)

        hl = jnp.zeros((SUBLANES, bw), F32)
        pp = jnp.ones((SUBLANES, bw), F32)
        hls, pps = [], []
        for j in range(seg):
            aj = a[j * SUBLANES:(j + 1) * SUBLANES]
            hl = aj * hl + u[j * SUBLANES:(j + 1) * SUBLANES]
            pp = aj * pp
            hls.append(hl)
            pps.append(pp)
        cin = hstate[0:1, cols]
        h_in = jnp.zeros((SUBLANES, bw), F32)
        for s in range(SUBLANES):
            h_in = jnp.where(sublane == s, cin, h_in)
            cin = hl[s:s + 1] + pp[s:s + 1] * cin
        hstate[:, cols] = jnp.broadcast_to(cin, (SUBLANES, bw))
        blocks.append(jnp.concatenate([hls[j] + pps[j] * h_in for j in range(seg)], axis=0))

    hs = jnp.concatenate(blocks, axis=1)
    y = (hs * _gelu(gate)).astype(BF16)
    _store_segment_rows(hs_ref[...] + _dot(y, wout_ref[...]), seg_out, stage, o_ref)


def _recurrent_block(h3, g, win, cw, cb, wa, ba, wx, bx, ap, wout,
                     seg_in=False, seg_out=False, tm=SEQ_TILE):
    B, S, D = h3.shape
    C = wout.shape[0]
    taps = cw.shape[0]
    unused = (SUBLANES, LANES)
    stage_shape = unused if (seg_in and seg_out) else _stage_shape(tm, D)
    hseg_shape = unused if seg_in else (tm, D)
    return pl.pallas_call(
        functools.partial(_rec_kernel, taps=taps, seg_in=seg_in, seg_out=seg_out),
        grid=(B, S // tm),
        in_specs=[
            pl.BlockSpec((None, tm, D), lambda b, i: (b, i, 0)),
            _resident((1, D)), _resident(win.shape), _resident(cw.shape), _resident((1, C)),
            _resident(wa.shape), _resident((1, C)), _resident(wx.shape), _resident((1, C)),
            _resident((1, C)), _resident(wout.shape),
        ],
        out_specs=pl.BlockSpec((None, tm, D), lambda b, i: (b, i, 0)),
        out_shape=jax.ShapeDtypeStruct((B, S, D), F32),
        scratch_shapes=[
            pltpu.VMEM(stage_shape, F32),
            pltpu.VMEM(hseg_shape, F32),
            pltpu.VMEM((taps - 1, SUBLANES, C), F32),
            pltpu.VMEM((SUBLANES, C), F32),
        ],
        compiler_params=_params(("arbitrary", "arbitrary")),
        name="recurrent_block",
    )(h3, g, win, cw, cb, wa, ba, wx, bx, ap, wout)


def _alibi_slopes(n):
    return jnp.exp2(-8.0 * jnp.arange(1, n + 1, dtype=F32) / n)


def _row(v):
    return v.reshape(1, -1).astype(F32)


def kernel(x, attn_norm, attn_w_in, attn_w_out, a_q_norm, a_k_norm, b_q_norm, b_k_norm, b_sub_norm,
           b_lam_q1, b_lam_k1, b_lam_q2, b_lam_k2, rec_norm, rec_w_in, rec_conv_w, rec_conv_b,
           rec_gate_a_w, rec_gate_a_b, rec_gate_x_w, rec_gate_x_b, rec_a_param, rec_w_out,
           ffn_norm, ffn_w_up, ffn_conv_w, ffn_conv_b, ffn_w_down):
    B, S, D = x.shape
    depth = ffn_norm.shape[0]
    slopes = _alibi_slopes(A_HEADS + B_HEADS)
    h = x
    h_is_seg = False
    for layer in range(depth):
        j = layer // 2
        attn = None
        if layer % 2 == 0:
            lam_init = 0.8 - 0.6 * math.exp(-0.3 * layer)
            scale = HEAD_DIM ** -0.5
            reps = SEC // HEAD_DIM
            ones = jnp.ones((SEC,), F32)
            head_gains = jnp.stack([
                jnp.tile(a_q_norm[j].astype(F32), reps) * (scale * LOG2E),
                jnp.tile(a_k_norm[j].astype(F32), reps), ones,
                jnp.tile(b_q_norm[j].astype(F32), reps) * (scale * LOG2E),
                jnp.tile(b_k_norm[j].astype(F32), reps), ones])
            proj = _attn_inproj(h.reshape(B * S, D), _row(attn_norm[j]),
                                attn_w_in[j].astype(BF16), head_gains)
            proj3 = proj.reshape(B, S, -1)
            oa = _dilated_attention(proj3, slopes[:A_HEADS] * LOG2E)
            lam_vecs = jnp.stack([b_lam_q1[j], b_lam_k1[j], b_lam_q2[j], b_lam_k2[j]]).astype(F32)
            ob = _diff_attention(proj3, slopes[A_HEADS:] * LOG2E, lam_vecs, _row(b_sub_norm[j]),
                                 lam_init)
            attn = (oa, ob, attn_w_out[j].astype(BF16))
        else:
            h = _recurrent_block(
                h, _row(rec_norm[j]), rec_w_in[j].astype(BF16), rec_conv_w[j].astype(F32),
                _row(rec_conv_b[j]), rec_gate_a_w[j].astype(BF16), _row(rec_gate_a_b[j]),
                rec_gate_x_w[j].astype(BF16), _row(rec_gate_x_b[j]), _row(rec_a_param[j]),
                rec_w_out[j].astype(BF16), seg_in=h_is_seg, seg_out=True)
            h_is_seg = True
        next_is_rec = layer + 1 < depth and (layer + 1) % 2 == 1
        h = _conv_ffn(h, _row(ffn_norm[layer]), ffn_w_up[layer].astype(BF16),
                      ffn_conv_w[layer].astype(F32), _row(ffn_conv_b[layer]),
                      ffn_w_down[layer].astype(BF16), attn=attn,
                      seg_in=h_is_seg, seg_out=next_is_rec)
        h_is_seg = next_is_rec
    return h
```

```python
import functools
import math

import numpy as np
import jax
import jax.numpy as jnp
from jax import lax
from jax.experimental import pallas as pl
from jax.experimental.pallas import tpu as pltpu

F32 = jnp.float32
BF16 = jnp.bfloat16

HEAD_DIM = 64
A_HEADS = 8
B_HEADS = 4
DILATED_CONFIGS = ((128, 1), (512, 4), (2048, 16))
BAND = 128
MAX_WINDOW = 2048
LRU_BLOCKS = 8
LRU_C = 8.0
NORM_EPS = 1e-6
NEG_INF = -1e30
LOG2E = math.log2(math.e)
LANES = 128
SUBLANES = 8
VMEM_LIMIT = 56 * 1024 * 1024


def _gelu(x):
    c = math.sqrt(2.0 / math.pi)
    return x * (0.5 * (1.0 + jnp.tanh(c * (x + 0.044715 * (x * x * x)))))


def _sigmoid(x):
    return 1.0 / (1.0 + jnp.exp(-x))


def _rms_rows(x, g):
    ms = jnp.mean(x * x, axis=-1, keepdims=True)
    return x * lax.rsqrt(ms + NORM_EPS) * g


def _dot(a, b):
    return jnp.dot(a, b, preferred_element_type=F32)


def _dot_nt(a, b):
    return lax.dot_general(a, b, (((1,), (1,)), ((), ())), preferred_element_type=F32)


def _params(sem):
    return pltpu.CompilerParams(dimension_semantics=sem, vmem_limit_bytes=VMEM_LIMIT)


SEC = 512
N_SEC = 6
MXU_TILE = 256


def _inproj_kernel(x_ref, g_ref, w_ref, hg_ref, p_ref, o_ref):
    xn = _rms_rows(x_ref[...], g_ref[...]).astype(BF16)
    for s in range(N_SEC):
        y = _dot(xn, w_ref[:, s * SEC:(s + 1) * SEC])
        if s % 3 == 2:
            o_ref[:, s * SEC:(s + 1) * SEC] = y.astype(BF16)
        else:
            y2 = (y * y).astype(BF16)
            ms = jnp.concatenate(
                [_dot(y2[:, c:c + MXU_TILE], p_ref[...]) for c in range(0, SEC, MXU_TILE)], axis=1)
            o_ref[:, s * SEC:(s + 1) * SEC] = (
                y * lax.rsqrt(ms + NORM_EPS) * hg_ref[s:s + 1, :]).astype(BF16)


def _attn_inproj(x2, g, w, head_gains, tm=512):
    T, D = x2.shape
    N = w.shape[1]
    blk = np.kron(np.eye(MXU_TILE // HEAD_DIM), np.full((HEAD_DIM, HEAD_DIM), 1.0 / HEAD_DIM))
    pmat = jnp.asarray(blk, dtype=BF16)
    return pl.pallas_call(
        _inproj_kernel,
        grid=(T // tm,),
        in_specs=[
            pl.BlockSpec((tm, D), lambda i: (i, 0)),
            pl.BlockSpec((1, D), lambda i: (0, 0)),
            pl.BlockSpec((D, N), lambda i: (0, 0)),
            pl.BlockSpec((N_SEC, SEC), lambda i: (0, 0)),
            pl.BlockSpec((MXU_TILE, MXU_TILE), lambda i: (0, 0)),
        ],
        out_specs=pl.BlockSpec((tm, N), lambda i: (i, 0)),
        out_shape=jax.ShapeDtypeStruct((T, N), BF16),
        compiler_params=_params(("parallel",)),
        name="attn_inproj",
    )(x2, g, w, head_gains, pmat)


ROW_GROUP = 16
ROW_PITCH = 24


def _spread_row(t):
    return (t // ROW_GROUP) * ROW_PITCH + t % ROW_GROUP


def _dilated_kernel(slopes_ref, q_ref, k_ref, v_ref, o_ref, qf, kf, vf, qp, kp, vp, ob, db, mb,
                    sb, bias_buf):
    assert [d % ROW_GROUP == 0 for _, d in DILATED_CONFIGS] == [False, False, True]
    S = q_ref.shape[0]
    pad = kf.shape[0] - S
    g = pl.program_id(1)
    qf[...] = q_ref[...].astype(F32)
    kf[0:pad, :] = jnp.zeros((pad, LANES), F32)
    vf[0:pad, :] = jnp.zeros((pad, LANES), F32)
    kf[pad:, :] = k_ref[...].astype(F32)
    vf[pad:, :] = v_ref[...].astype(F32)
    kp[0:_spread_row(pad), :] = jnp.zeros((_spread_row(pad), LANES), F32)
    vp[0:_spread_row(pad), :] = jnp.zeros((_spread_row(pad), LANES), F32)

    def spread(grp, carry):
        src = pl.ds(pl.multiple_of(grp * ROW_GROUP, ROW_GROUP), ROW_GROUP)
        dst = pl.ds(pl.multiple_of(grp * ROW_PITCH, SUBLANES), ROW_GROUP)
        dst_kv = pl.ds(pl.multiple_of(grp * ROW_PITCH + _spread_row(pad), SUBLANES), ROW_GROUP)
        qp[dst, :] = q_ref[src, :].astype(F32)
        kp[dst_kv, :] = k_ref[src, :].astype(F32)
        vp[dst_kv, :] = v_ref[src, :].astype(F32)
        return carry
    lax.fori_loop(0, S // ROW_GROUP, spread, 0, unroll=8)

    lo = lax.broadcasted_iota(jnp.int32, (BAND, LANES), 1) < HEAD_DIM
    ii = lax.broadcasted_iota(jnp.int32, (BAND, 2 * BAND), 0)
    jj = lax.broadcasted_iota(jnp.int32, (BAND, 2 * BAND), 1)
    delta = ii + BAND - jj
    in_band = (delta >= 0) & (delta <= BAND)
    prev_half = lax.broadcasted_iota(jnp.int32, (2 * BAND, 2 * BAND), 1) < BAND
    sl0 = slopes_ref[2 * g]
    sl1 = slopes_ref[2 * g + 1]
    ones = jnp.ones((2 * BAND, LANES), BF16)

    for bi, (window, d) in enumerate(DILATED_CONFIGS):
        span = BAND * d
        nbs = S // span
        dist = (delta * d).astype(F32)
        bias = jnp.concatenate(
            [jnp.where(in_band, -sl0 * dist, NEG_INF),
             jnp.where(in_band, -sl1 * dist, NEG_INF)], axis=0)
        bias_buf[2 * bi] = bias
        bias_buf[2 * bi + 1] = jnp.where(prev_half, NEG_INF, bias)

        def body(blk, carry, d=d, span=span, nbs=nbs, bi=bi):
            r = blk // nbs
            nb = blk - r * nbs
            start = nb * span + r
            if d % ROW_GROUP == 0:
                stride = d // ROW_GROUP * ROW_PITCH
                q_row = nb * _spread_row(span) + r
                kv_row = q_row + _spread_row(pad) - _spread_row(span)
                q = qp[pl.ds(q_row, BAND, stride=stride), :].astype(BF16)
                k = kp[pl.ds(kv_row, 2 * BAND, stride=stride), :].astype(BF16)
                v = vp[pl.ds(kv_row, 2 * BAND, stride=stride), :].astype(BF16)
            else:
                q = qf[pl.ds(start, BAND, stride=d), :].astype(BF16)
                k = kf[pl.ds(pad + start - span, 2 * BAND, stride=d), :].astype(BF16)
                v = vf[pl.ds(pad + start - span, 2 * BAND, stride=d), :].astype(BF16)
            zero = jnp.zeros_like(q)
            q2 = jnp.concatenate([jnp.where(lo, q, zero), jnp.where(lo, zero, q)], axis=0)
            s = _dot_nt(q2, k) + bias_buf[2 * bi + jnp.where(nb == 0, 1, 0)]
            m = jnp.max(s, axis=1, keepdims=True)
            e = jnp.exp2(s - m)
            pv = _dot(e.astype(BF16), jnp.concatenate([v, ones], axis=1))
            num = jnp.where(lo, pv[:BAND, :LANES], pv[BAND:, :LANES])
            den = jnp.where(lo, pv[:BAND, LANES:], pv[BAND:, LANES:])
            top = jnp.where(lo, jnp.broadcast_to(m[:BAND], (BAND, LANES)),
                            jnp.broadcast_to(m[BAND:], (BAND, LANES)))
            if d % ROW_GROUP == 0:
                rows = pl.ds(q_row, BAND, stride=stride)
                sb[0, rows, :] = num
                sb[1, rows, :] = den
                sb[2, rows, :] = top
            else:
                rows = pl.ds(start, BAND, stride=d)
                ob[bi, rows, :] = num
                db[bi, rows, :] = den
                mb[bi, rows, :] = top
            return carry

        lax.fori_loop(0, S // BAND, body, 0, unroll=32)

    rows = 512
    def mix(c, carry):
        sl = pl.ds(pl.multiple_of(c * rows, rows), rows)
        base = pl.multiple_of(c * _spread_row(rows), SUBLANES)

        def spread_chunk(t):
            return jnp.concatenate(
                [sb[t, pl.ds(base + gi * ROW_PITCH, ROW_GROUP), :] for gi in range(rows // ROW_GROUP)],
                axis=0)
        m0, m1, m2 = mb[0, sl, :], mb[1, sl, :], spread_chunk(2)
        m = jnp.maximum(jnp.maximum(m0, m1), m2)
        w0, w1, w2 = jnp.exp2(m0 - m), jnp.exp2(m1 - m), jnp.exp2(m2 - m)
        num = w0 * ob[0, sl, :] + w1 * ob[1, sl, :] + w2 * spread_chunk(0)
        den = w0 * db[0, sl, :] + w1 * db[1, sl, :] + w2 * spread_chunk(1)
        o_ref[sl, :] = (num / den).astype(o_ref.dtype)
        return carry
    lax.fori_loop(0, S // rows, mix, 0)


def _dilated_attention(proj3, slopes):
    B, S, _ = proj3.shape
    assert S % MAX_WINDOW == 0
    pairs = A_HEADS // 2
    blk = lambda off: pl.BlockSpec((None, S, LANES), lambda b, g, off=off: (b, 0, off + g))
    return pl.pallas_call(
        _dilated_kernel,
        grid=(B, pairs),
        in_specs=[
            pl.BlockSpec(memory_space=pltpu.SMEM),
            blk(0), blk(pairs), blk(2 * pairs),
        ],
        out_specs=pl.BlockSpec((None, S, LANES), lambda b, g: (b, 0, g)),
        out_shape=jax.ShapeDtypeStruct((B, S, A_HEADS * HEAD_DIM), BF16),
        scratch_shapes=[
            pltpu.VMEM((S, LANES), F32),
            pltpu.VMEM((MAX_WINDOW + S, LANES), F32),
            pltpu.VMEM((MAX_WINDOW + S, LANES), F32),
            pltpu.VMEM((_spread_row(S), LANES), F32),
            pltpu.VMEM((_spread_row(MAX_WINDOW + S), LANES), F32),
            pltpu.VMEM((_spread_row(MAX_WINDOW + S), LANES), F32),
            pltpu.VMEM((len(DILATED_CONFIGS) - 1, S, LANES), F32),
            pltpu.VMEM((len(DILATED_CONFIGS) - 1, S, LANES), F32),
            pltpu.VMEM((len(DILATED_CONFIGS) - 1, S, LANES), F32),
            pltpu.VMEM((3, _spread_row(S), LANES), F32),
            pltpu.VMEM((2 * len(DILATED_CONFIGS), 2 * BAND, 2 * BAND), F32),
        ],
        compiler_params=_params(("parallel", "parallel")),
        name="dilated_attn",
    )(slopes, proj3, proj3, proj3)


TQ = 512
KV_UNROLL = 2
DIFF_HEADS = 4


def _diff_kernel(slopes_ref, q_ref, k_ref, v_ref, lam_ref, sg_ref, o_ref,
                 m_ref, acc_ref, *, lam_init):
    hg = pl.program_id(1)
    qi = pl.program_id(2)
    lo = lax.broadcasted_iota(jnp.int32, (TQ, LANES), 1) < HEAD_DIM
    heads = range(DIFF_HEADS)
    cols = [slice(hh * LANES, (hh + 1) * LANES) for hh in heads]
    slopes = [slopes_ref[hg * DIFF_HEADS + hh] for hh in heads]
    q2 = []
    for hh in heads:
        q = q_ref[:, cols[hh]]
        zero = jnp.zeros_like(q)
        q2.append(jnp.concatenate([jnp.where(lo, q, zero), jnp.where(lo, zero, q)], axis=0))

    m_ref[...] = jnp.full(m_ref.shape, NEG_INF, F32)
    acc_ref[...] = jnp.zeros(acc_ref.shape, F32)

    def update(hh, key0, nk, row_blocks, masked):
        ks = pl.ds(pl.multiple_of(key0, nk), nk)
        k = k_ref[ks, cols[hh]]
        v1 = jnp.concatenate([v_ref[ks, cols[hh]], jnp.ones((nk, LANES), BF16)], axis=1)
        qs = jnp.concatenate([q2[hh][r0:r0 + rn] for r0, rn in row_blocks], axis=0)
        nrows = qs.shape[0]
        kpos = lax.broadcasted_iota(jnp.int32, (1, nk), 1).astype(F32) + key0.astype(F32)
        s = _dot_nt(qs, k) + slopes[hh] * kpos
        if masked:
            rn = row_blocks[0][1]
            rel = (lax.broadcasted_iota(jnp.int32, (nrows, nk), 0) % rn
                   - lax.broadcasted_iota(jnp.int32, (nrows, nk), 1))
            s = jnp.where(rel >= 0, s, NEG_INF)
        m_old = jnp.concatenate([m_ref[hh, r0:r0 + rn] for r0, rn in row_blocks], axis=0)
        a_old = jnp.concatenate([acc_ref[hh, r0:r0 + rn] for r0, rn in row_blocks], axis=0)
        m_new = jnp.maximum(m_old, jnp.max(s, axis=1, keepdims=True))
        alpha = jnp.exp2(m_old - m_new)
        e = jnp.exp2(s - jnp.concatenate([m_new] * (nk // LANES), axis=1))
        a_new = jnp.concatenate([alpha, alpha], axis=1) * a_old + _dot(e.astype(BF16), v1)
        at = 0
        for r0, rn in row_blocks:
            m_ref[hh, r0:r0 + rn] = m_new[at:at + rn]
            acc_ref[hh, r0:r0 + rn] = a_new[at:at + rn]
            at += rn

    half = TQ // 2

    def step(j, masked, hh):
        if not masked:
            update(hh, j * TQ, TQ, [(0, 2 * TQ)], False)
            return
        update(hh, j * TQ, half, [(0, TQ), (TQ, TQ)], True)
        update(hh, j * TQ + half, half, [(half, half), (TQ + half, half)], True)

    def body(jq, carry):
        for u in range(KV_UNROLL):
            for hh in heads:
                step(KV_UNROLL * jq + u, False, hh)
        return carry
    nq = qi // KV_UNROLL
    lax.fori_loop(0, nq, body, 0)

    rem = qi - nq * KV_UNROLL
    for r in range(KV_UNROLL):
        @pl.when(rem == r)
        def _(r=r):
            for u in range(r):
                for hh in heads:
                    step(qi - r + u, False, hh)
            for hh in heads:
                step(qi, True, hh)

    lq = lam_ref[...]
    lam = (jnp.exp(jnp.sum(lq[0:1] * lq[1:2], axis=1, keepdims=True))
           - jnp.exp(jnp.sum(lq[2:3] * lq[3:4], axis=1, keepdims=True)) + lam_init)
    for hh in heads:
        on = acc_ref[hh, :, 0:LANES] / acc_ref[hh, :, LANES:]
        o = on[:TQ] - lam * on[TQ:]
        o = _rms_rows(o, sg_ref[...]) * (1.0 - lam_init)
        o_ref[:, cols[hh]] = o.astype(o_ref.dtype)


def _diff_attention(proj3, slopes, lam_vecs, sub_gain, lam_init):
    B, S, _ = proj3.shape
    width = DIFF_HEADS * LANES
    groups = B_HEADS // DIFF_HEADS
    qoff = 3 * A_HEADS * HEAD_DIM // width
    koff = qoff + groups
    voff = koff + groups
    return pl.pallas_call(
        functools.partial(_diff_kernel, lam_init=lam_init),
        grid=(B, groups, S // TQ),
        in_specs=[
            pl.BlockSpec(memory_space=pltpu.SMEM),
            pl.BlockSpec((None, TQ, width), lambda b, h, i: (b, i, qoff + h)),
            pl.BlockSpec((None, S, width), lambda b, h, i: (b, 0, koff + h)),
            pl.BlockSpec((None, S, width), lambda b, h, i: (b, 0, voff + h)),
            pl.BlockSpec((4, HEAD_DIM), lambda b, h, i: (0, 0)),
            pl.BlockSpec((1, 2 * HEAD_DIM), lambda b, h, i: (0, 0)),
        ],
        out_specs=pl.BlockSpec((None, TQ, width), lambda b, h, i: (b, i, h)),
        out_shape=jax.ShapeDtypeStruct((B, S, B_HEADS * 2 * HEAD_DIM), BF16),
        scratch_shapes=[
            pltpu.VMEM((DIFF_HEADS, 2 * TQ, LANES), F32),
            pltpu.VMEM((DIFF_HEADS, 2 * TQ, 2 * LANES), F32),
        ],
        compiler_params=_params(("parallel", "parallel", "parallel")),
        name="diff_attn",
    )(slopes, proj3, proj3, proj3, lam_vecs, sub_gain)


FFN_CHUNK = 512


def _seg_pitch(seg):
    p = seg // SUBLANES + 1
    return SUBLANES * (p if p % 2 else p + 1)


def _to_segment_rows(src, stage, dst_ref):
    tm, C = src.shape
    seg = tm // SUBLANES
    pitch = _seg_pitch(seg)
    for n in range(C // LANES):
        cols = slice(n * LANES, (n + 1) * LANES)
        for s in range(SUBLANES):
            stage[n, s * pitch:s * pitch + seg, :] = src[s * seg:(s + 1) * seg, cols]
    for j in range(seg):
        dst_ref[j * SUBLANES:(j + 1) * SUBLANES, :] = jnp.concatenate(
            [stage[n, pl.ds(j, SUBLANES, stride=pitch), :] for n in range(C // LANES)], axis=1)


def _from_segment_rows(val, stage, dst_ref):
    tm, C = val.shape
    seg = tm // SUBLANES
    pitch = _seg_pitch(seg)
    for n in range(C // LANES):
        cols = slice(n * LANES, (n + 1) * LANES)
        for j in range(seg):
            stage[n, pl.ds(j, SUBLANES, stride=pitch), :] = val[j * SUBLANES:(j + 1) * SUBLANES, cols]
        for s in range(SUBLANES):
            dst_ref[s * seg:(s + 1) * seg, cols] = stage[n, s * pitch:s * pitch + seg, :]


def _segment_rows_ref(src, seg_in, stage, hseg):
    if seg_in:
        return src
    _to_segment_rows(src, stage, hseg)
    return hseg


def _store_segment_rows(val, seg_out, stage, o_ref):
    if seg_out:
        o_ref[...] = val
    else:
        _from_segment_rows(val, stage, o_ref)


def _segment_conv(u, carry_ref, cols, w_ref, b_ref, taps):
    tm = u.shape[0]
    first_sublane = lax.broadcasted_iota(jnp.int32, (SUBLANES, u.shape[1]), 0) == 0
    wrapped = []
    for i in range(1, taps):
        tail = u[tm - i * SUBLANES:tm - (i - 1) * SUBLANES]
        wrapped.append(jnp.where(first_sublane,
                                 pltpu.roll(carry_ref[i - 1, :, cols], 1, 0),
                                 pltpu.roll(tail, 1, 0)))
        carry_ref[i - 1, :, cols] = tail
    out = b_ref[:, cols] + w_ref[taps - 1:taps, cols] * u
    for k in range(1, taps):
        shifted = jnp.concatenate(wrapped[k - 1::-1] + [u[:tm - k * SUBLANES]], axis=0)
        out = out + w_ref[taps - 1 - k:taps - k, cols] * shifted
    return out


def _ffn_kernel(*refs, taps, with_attn, seg_in, seg_out):
    if with_attn:
        x_ref, a_ref, b_ref, wo_ref, *refs = refs
        na = a_ref.shape[1]
        h = x_ref[...] + _dot(a_ref[...], wo_ref[0:na, :]) + _dot(b_ref[...], wo_ref[na:, :])
    else:
        h, *refs = refs
    g_ref, wup_ref, cw_ref, cb_ref, wdn_ref, o_ref, stage, hseg, carry, act = refs
    F = wdn_ref.shape[0]

    @pl.when(pl.program_id(1) == 0)
    def _():
        carry[...] = jnp.zeros(carry.shape, F32)

    hs_ref = _segment_rows_ref(h, seg_in, stage, hseg)
    xn = _rms_rows(hs_ref[...], g_ref[...]).astype(BF16)
    for c in range(F // FFN_CHUNK):
        gc = slice(c * FFN_CHUNK, (c + 1) * FFN_CHUNK)
        vc = slice(F + c * FFN_CHUNK, F + (c + 1) * FFN_CHUNK)
        gg = _segment_conv(_dot(xn, wup_ref[:, gc]), carry, gc, cw_ref, cb_ref, taps)
        vv = _segment_conv(_dot(xn, wup_ref[:, vc]), carry, vc, cw_ref, cb_ref, taps)
        act[:, gc] = (_gelu(gg) * vv).astype(BF16)
    _store_segment_rows(hs_ref[...] + _dot(act[...], wdn_ref[...]), seg_out, stage, o_ref)


def _stage_shape(tm, C):
    return (C // LANES, SUBLANES * _seg_pitch(tm // SUBLANES), LANES)


def _resident(shape):
    return pl.BlockSpec(shape, lambda b, i: (0,) * len(shape), pipeline_mode=pl.Buffered(1))


SEQ_TILE = 512


def _conv_ffn(h3, g, wup, cw, cb, wdn, attn=None, seg_in=False, seg_out=False, tm=SEQ_TILE):
    assert not (seg_in and attn is not None)
    B, S, D = h3.shape
    F2 = wup.shape[1]
    taps = cw.shape[0]
    row_tile = lambda width: pl.BlockSpec((None, tm, width), lambda b, i: (b, i, 0))
    attn_args, attn_specs = (), []
    if attn is not None:
        oa, ob, wo = attn
        attn_args = (oa, ob, wo)
        attn_specs = [row_tile(oa.shape[2]), row_tile(ob.shape[2]), _resident(wo.shape)]
    return pl.pallas_call(
        functools.partial(_ffn_kernel, taps=taps, with_attn=attn is not None,
                          seg_in=seg_in, seg_out=seg_out),
        grid=(B, S // tm),
        in_specs=[
            row_tile(D),
            *attn_specs,
            _resident((1, D)),
            _resident(wup.shape),
            _resident(cw.shape),
            _resident((1, F2)),
            _resident(wdn.shape),
        ],
        out_specs=row_tile(D),
        out_shape=jax.ShapeDtypeStruct((B, S, D), F32),
        scratch_shapes=[
            pltpu.VMEM(_stage_shape(tm, D), F32),
            pltpu.VMEM((tm, D), F32),
            pltpu.VMEM((taps - 1, SUBLANES, F2), F32),
            pltpu.VMEM((tm, F2 // 2), BF16),
        ],
        compiler_params=_params(("arbitrary", "arbitrary")),
        name="conv_ffn",
    )(h3, *attn_args, g, wup, cw, cb, wdn)


def _rec_kernel(h_ref, g_ref, win_ref, cw_ref, cb_ref, wa_ref, ba_ref, wx_ref, bx_ref,
                ap_ref, wout_ref, o_ref, stage, hseg, carry, hstate, *, taps, seg_in, seg_out):
    tm, C = h_ref.shape
    seg = tm // SUBLANES
    bw = C // LRU_BLOCKS

    @pl.when(pl.program_id(1) == 0)
    def _():
        carry[...] = jnp.zeros(carry.shape, F32)
        hstate[...] = jnp.zeros(hstate.shape, F32)

    hs_ref = _segment_rows_ref(h_ref, seg_in, stage, hseg)
    xn = _rms_rows(hs_ref[...], g_ref[...]).astype(BF16)
    gate = _dot(xn, win_ref[:, 0:C])
    xr_all = _segment_conv(_dot(xn, win_ref[:, C:]), carry, slice(0, C), cw_ref, cb_ref, taps)

    ap = ap_ref[...]
    decay = -LRU_C * (jnp.maximum(-ap, 0.0) + jnp.log1p(jnp.exp(-jnp.abs(ap))))
    sublane = lax.broadcasted_iota(jnp.int32, (SUBLANES, bw), 0)
    blocks = []
    for n in range(LRU_BLOCKS):
        cols = slice(n * bw, (n + 1) * bw)
        xr = xr_all[:, cols]
        xb = xr.astype(BF16)
        r = _sigmoid(_dot(xb, wa_ref[n]) + ba_ref[:, cols])
        i = _sigmoid(_dot(xb, wx_ref[n]) + bx_ref[:, cols])
        log_a = decay[:, cols] * r
        a = jnp.exp(log_a)
        u = jnp.sqrt(-jnp.tanh(log_a) * (1.0 + a * a)) * (i * xr)

        hl = jnp.zeros((SUBLANES, bw), F32)
        pp = jnp.ones((SUBLANES, bw), F32)
        hls, pps = [], []
        for j in range(seg):
            aj = a[j * SUBLANES:(j + 1) * SUBLANES]
            hl = aj * hl + u[j * SUBLANES:(j + 1) * SUBLANES]
            pp = aj * pp
            hls.append(hl)
            pps.append(pp)
        cin = hstate[0:1, cols]
        h_in = jnp.zeros((SUBLANES, bw), F32)
        for s in range(SUBLANES):
            h_in = jnp.where(sublane == s, cin, h_in)
            cin = hl[s:s + 1] + pp[s:s + 1] * cin
        hstate[:, cols] = jnp.broadcast_to(cin, (SUBLANES, bw))
        blocks.append(jnp.concatenate([hls[j] + pps[j] * h_in for j in range(seg)], axis=0))

    hs = jnp.concatenate(blocks, axis=1)
    y = (hs * _gelu(gate)).astype(BF16)
    _store_segment_rows(hs_ref[...] + _dot(y, wout_ref[...]), seg_out, stage, o_ref)


def _recurrent_block(h3, g, win, cw, cb, wa, ba, wx, bx, ap, wout,
                     seg_in=False, seg_out=False, tm=SEQ_TILE):
    B, S, D = h3.shape
    C = wout.shape[0]
    taps = cw.shape[0]
    unused = (SUBLANES, LANES)
    stage_shape = unused if (seg_in and seg_out) else _stage_shape(tm, D)
    hseg_shape = unused if seg_in else (tm, D)
    return pl.pallas_call(
        functools.partial(_rec_kernel, taps=taps, seg_in=seg_in, seg_out=seg_out),
        grid=(B, S // tm),
        in_specs=[
            pl.BlockSpec((None, tm, D), lambda b, i: (b, i, 0)),
            _resident((1, D)), _resident(win.shape), _resident(cw.shape), _resident((1, C)),
            _resident(wa.shape), _resident((1, C)), _resident(wx.shape), _resident((1, C)),
            _resident((1, C)), _resident(wout.shape),
        ],
        out_specs=pl.BlockSpec((None, tm, D), lambda b, i: (b, i, 0)),
        out_shape=jax.ShapeDtypeStruct((B, S, D), F32),
        scratch_shapes=[
            pltpu.VMEM(stage_shape, F32),
            pltpu.VMEM(hseg_shape, F32),
            pltpu.VMEM((taps - 1, SUBLANES, C), F32),
            pltpu.VMEM((SUBLANES, C), F32),
        ],
        compiler_params=_params(("arbitrary", "arbitrary")),
        name="recurrent_block",
    )(h3, g, win, cw, cb, wa, ba, wx, bx, ap, wout)


def _alibi_slopes(n):
    return jnp.exp2(-8.0 * jnp.arange(1, n + 1, dtype=F32) / n)


def _row(v):
    return v.reshape(1, -1).astype(F32)


def kernel(x, attn_norm, attn_w_in, attn_w_out, a_q_norm, a_k_norm, b_q_norm, b_k_norm, b_sub_norm,
           b_lam_q1, b_lam_k1, b_lam_q2, b_lam_k2, rec_norm, rec_w_in, rec_conv_w, rec_conv_b,
           rec_gate_a_w, rec_gate_a_b, rec_gate_x_w, rec_gate_x_b, rec_a_param, rec_w_out,
           ffn_norm, ffn_w_up, ffn_conv_w, ffn_conv_b, ffn_w_down):
    B, S, D = x.shape
    depth = ffn_norm.shape[0]
    slopes = _alibi_slopes(A_HEADS + B_HEADS)
    h = x
    h_is_seg = False
    for layer in range(depth):
        j = layer // 2
        attn = None
        if layer % 2 == 0:
            lam_init = 0.8 - 0.6 * math.exp(-0.3 * layer)
            scale = HEAD_DIM ** -0.5
            reps = SEC // HEAD_DIM
            ones = jnp.ones((SEC,), F32)
            head_gains = jnp.stack([
                jnp.tile(a_q_norm[j].astype(F32), reps) * (scale * LOG2E),
                jnp.tile(a_k_norm[j].astype(F32), reps), ones,
                jnp.tile(b_q_norm[j].astype(F32), reps) * (scale * LOG2E),
                jnp.tile(b_k_norm[j].astype(F32), reps), ones])
            proj = _attn_inproj(h.reshape(B * S, D), _row(attn_norm[j]),
                                attn_w_in[j].astype(BF16), head_gains)
            proj3 = proj.reshape(B, S, -1)
            oa = _dilated_attention(proj3, slopes[:A_HEADS] * LOG2E)
            lam_vecs = jnp.stack([b_lam_q1[j], b_lam_k1[j], b_lam_q2[j], b_lam_k2[j]]).astype(F32)
            ob = _diff_attention(proj3, slopes[A_HEADS:] * LOG2E, lam_vecs, _row(b_sub_norm[j]),
                                 lam_init)
            attn = (oa, ob, attn_w_out[j].astype(BF16))
        else:
            h = _recurrent_block(
                h, _row(rec_norm[j]), rec_w_in[j].astype(BF16), rec_conv_w[j].astype(F32),
                _row(rec_conv_b[j]), rec_gate_a_w[j].astype(BF16), _row(rec_gate_a_b[j]),
                rec_gate_x_w[j].astype(BF16), _row(rec_gate_x_b[j]), _row(rec_a_param[j]),
                rec_w_out[j].astype(BF16), seg_in=h_is_seg, seg_out=True)
            h_is_seg = True
        next_is_rec = layer + 1 < depth and (layer + 1) % 2 == 1
        h = _conv_ffn(h, _row(ffn_norm[layer]), ffn_w_up[layer].astype(BF16),
                      ffn_conv_w[layer].astype(F32), _row(ffn_conv_b[layer]),
                      ffn_w_down[layer].astype(BF16), attn=attn,
                      seg_in=h_is_seg, seg_out=next_is_rec)
        h_is_seg = next_is_rec
    return h
```

```python
import functools
import math

import numpy as np
import jax
import jax.numpy as jnp
from jax import lax
from jax.experimental import pallas as pl
from jax.experimental.pallas import tpu as pltpu

F32 = jnp.float32
BF16 = jnp.bfloat16

HEAD_DIM = 64
A_HEADS = 8
B_HEADS = 4
DILATED_CONFIGS = ((128, 1), (512, 4), (2048, 16))
BAND = 128
MAX_WINDOW = 2048
LRU_BLOCKS = 8
LRU_C = 8.0
NORM_EPS = 1e-6
NEG_INF = -1e30
LOG2E = math.log2(math.e)
LANES = 128
SUBLANES = 8
VMEM_LIMIT = 56 * 1024 * 1024


def _gelu(x):
    c = math.sqrt(2.0 / math.pi)
    return x * (0.5 * (1.0 + jnp.tanh(c * (x + 0.044715 * (x * x * x)))))


def _sigmoid(x):
    return 1.0 / (1.0 + jnp.exp(-x))


def _rms_rows(x, g):
    ms = jnp.mean(x * x, axis=-1, keepdims=True)
    return x * lax.rsqrt(ms + NORM_EPS) * g


def _dot(a, b):
    return jnp.dot(a, b, preferred_element_type=F32)


def _dot_nt(a, b):
    return lax.dot_general(a, b, (((1,), (1,)), ((), ())), preferred_element_type=F32)


def _params(sem):
    return pltpu.CompilerParams(dimension_semantics=sem, vmem_limit_bytes=VMEM_LIMIT)


SEC = 512
N_SEC = 6
MXU_TILE = 256


def _inproj_kernel(x_ref, g_ref, w_ref, hg_ref, p_ref, o_ref):
    xn = _rms_rows(x_ref[...], g_ref[...]).astype(BF16)
    for s in range(N_SEC):
        y = _dot(xn, w_ref[:, s * SEC:(s + 1) * SEC])
        if s % 3 == 2:
            o_ref[:, s * SEC:(s + 1) * SEC] = y.astype(BF16)
        else:
            y2 = (y * y).astype(BF16)
            ms = jnp.concatenate(
                [_dot(y2[:, c:c + MXU_TILE], p_ref[...]) for c in range(0, SEC, MXU_TILE)], axis=1)
            o_ref[:, s * SEC:(s + 1) * SEC] = (
                y * lax.rsqrt(ms + NORM_EPS) * hg_ref[s:s + 1, :]).astype(BF16)


def _attn_inproj(x2, g, w, head_gains, tm=512):
    T, D = x2.shape
    N = w.shape[1]
    blk = np.kron(np.eye(MXU_TILE // HEAD_DIM), np.full((HEAD_DIM, HEAD_DIM), 1.0 / HEAD_DIM))
    pmat = jnp.asarray(blk, dtype=BF16)
    return pl.pallas_call(
        _inproj_kernel,
        grid=(T // tm,),
        in_specs=[
            pl.BlockSpec((tm, D), lambda i: (i, 0)),
            pl.BlockSpec((1, D), lambda i: (0, 0)),
            pl.BlockSpec((D, N), lambda i: (0, 0)),
            pl.BlockSpec((N_SEC, SEC), lambda i: (0, 0)),
            pl.BlockSpec((MXU_TILE, MXU_TILE), lambda i: (0, 0)),
        ],
        out_specs=pl.BlockSpec((tm, N), lambda i: (i, 0)),
        out_shape=jax.ShapeDtypeStruct((T, N), BF16),
        compiler_params=_params(("parallel",)),
        name="attn_inproj",
    )(x2, g, w, head_gains, pmat)


ROW_GROUP = 16
ROW_PITCH = 24


def _spread_row(t):
    return (t // ROW_GROUP) * ROW_PITCH + t % ROW_GROUP


def _dilated_kernel(slopes_ref, q_ref, k_ref, v_ref, o_ref, qf, kf, vf, qp, kp, vp, ob, db, mb,
                    sb, bias_buf):
    assert [d % ROW_GROUP == 0 for _, d in DILATED_CONFIGS] == [False, False, True]
    S = q_ref.shape[0]
    pad = kf.shape[0] - S
    g = pl.program_id(1)
    qf[...] = q_ref[...].astype(F32)
    kf[0:pad, :] = jnp.zeros((pad, LANES), F32)
    vf[0:pad, :] = jnp.zeros((pad, LANES), F32)
    kf[pad:, :] = k_ref[...].astype(F32)
    vf[pad:, :] = v_ref[...].astype(F32)
    kp[0:_spread_row(pad), :] = jnp.zeros((_spread_row(pad), LANES), F32)
    vp[0:_spread_row(pad), :] = jnp.zeros((_spread_row(pad), LANES), F32)

    def spread(grp, carry):
        src = pl.ds(pl.multiple_of(grp * ROW_GROUP, ROW_GROUP), ROW_GROUP)
        dst = pl.ds(pl.multiple_of(grp * ROW_PITCH, SUBLANES), ROW_GROUP)
        dst_kv = pl.ds(pl.multiple_of(grp * ROW_PITCH + _spread_row(pad), SUBLANES), ROW_GROUP)
        qp[dst, :] = q_ref[src, :].astype(F32)
        kp[dst_kv, :] = k_ref[src, :].astype(F32)
        vp[dst_kv, :] = v_ref[src, :].astype(F32)
        return carry
    lax.fori_loop(0, S // ROW_GROUP, spread, 0, unroll=8)

    lo = lax.broadcasted_iota(jnp.int32, (BAND, LANES), 1) < HEAD_DIM
    ii = lax.broadcasted_iota(jnp.int32, (BAND, 2 * BAND), 0)
    jj = lax.broadcasted_iota(jnp.int32, (BAND, 2 * BAND), 1)
    delta = ii + BAND - jj
    in_band = (delta >= 0) & (delta <= BAND)
    prev_half = lax.broadcasted_iota(jnp.int32, (2 * BAND, 2 * BAND), 1) < BAND
    sl0 = slopes_ref[2 * g]
    sl1 = slopes_ref[2 * g + 1]
    ones = jnp.ones((2 * BAND, LANES), BF16)

    for bi, (window, d) in enumerate(DILATED_CONFIGS):
        span = BAND * d
        nbs = S // span
        dist = (delta * d).astype(F32)
        bias = jnp.concatenate(
            [jnp.where(in_band, -sl0 * dist, NEG_INF),
             jnp.where(in_band, -sl1 * dist, NEG_INF)], axis=0)
        bias_buf[2 * bi] = bias
        bias_buf[2 * bi + 1] = jnp.where(prev_half, NEG_INF, bias)

        def body(blk, carry, d=d, span=span, nbs=nbs, bi=bi):
            r = blk // nbs
            nb = blk - r * nbs
            start = nb * span + r
            if d % ROW_GROUP == 0:
                stride = d // ROW_GROUP * ROW_PITCH
                q_row = nb * _spread_row(span) + r
                kv_row = q_row + _spread_row(pad) - _spread_row(span)
                q = qp[pl.ds(q_row, BAND, stride=stride), :].astype(BF16)
                k = kp[pl.ds(kv_row, 2 * BAND, stride=stride), :].astype(BF16)
                v = vp[pl.ds(kv_row, 2 * BAND, stride=stride), :].astype(BF16)
            else:
                q = qf[pl.ds(start, BAND, stride=d), :].astype(BF16)
                k = kf[pl.ds(pad + start - span, 2 * BAND, stride=d), :].astype(BF16)
                v = vf[pl.ds(pad + start - span, 2 * BAND, stride=d), :].astype(BF16)
            zero = jnp.zeros_like(q)
            q2 = jnp.concatenate([jnp.where(lo, q, zero), jnp.where(lo, zero, q)], axis=0)
            s = _dot_nt(q2, k) + bias_buf[2 * bi + jnp.where(nb == 0, 1, 0)]
            m = jnp.max(s, axis=1, keepdims=True)
            e = jnp.exp2(s - m)
            pv = _dot(e.astype(BF16), jnp.concatenate([v, ones], axis=1))
            num = jnp.where(lo, pv[:BAND, :LANES], pv[BAND:, :LANES])
            den = jnp.where(lo, pv[:BAND, LANES:], pv[BAND:, LANES:])
            top = jnp.where(lo, jnp.broadcast_to(m[:BAND], (BAND, LANES)),
                            jnp.broadcast_to(m[BAND:], (BAND, LANES)))
            if d % ROW_GROUP == 0:
                rows = pl.ds(q_row, BAND, stride=stride)
                sb[0, rows, :] = num
                sb[1, rows, :] = den
                sb[2, rows, :] = top
            else:
                rows = pl.ds(start, BAND, stride=d)
                ob[bi, rows, :] = num
                db[bi, rows, :] = den
                mb[bi, rows, :] = top
            return carry

        lax.fori_loop(0, S // BAND, body, 0, unroll=32)

    rows = 512
    def mix(c, carry):
        sl = pl.ds(pl.multiple_of(c * rows, rows), rows)
        base = pl.multiple_of(c * _spread_row(rows), SUBLANES)

        def spread_chunk(t):
            return jnp.concatenate(
                [sb[t, pl.ds(base + gi * ROW_PITCH, ROW_GROUP), :] for gi in range(rows // ROW_GROUP)],
                axis=0)
        m0, m1, m2 = mb[0, sl, :], mb[1, sl, :], spread_chunk(2)
        m = jnp.maximum(jnp.maximum(m0, m1), m2)
        w0, w1, w2 = jnp.exp2(m0 - m), jnp.exp2(m1 - m), jnp.exp2(m2 - m)
        num = w0 * ob[0, sl, :] + w1 * ob[1, sl, :] + w2 * spread_chunk(0)
        den = w0 * db[0, sl, :] + w1 * db[1, sl, :] + w2 * spread_chunk(1)
        o_ref[sl, :] = (num / den).astype(o_ref.dtype)
        return carry
    lax.fori_loop(0, S // rows, mix, 0)


def _dilated_attention(proj3, slopes):
    B, S, _ = proj3.shape
    assert S % MAX_WINDOW == 0
    pairs = A_HEADS // 2
    blk = lambda off: pl.BlockSpec((None, S, LANES), lambda b, g, off=off: (b, 0, off + g))
    return pl.pallas_call(
        _dilated_kernel,
        grid=(B, pairs),
        in_specs=[
            pl.BlockSpec(memory_space=pltpu.SMEM),
            blk(0), blk(pairs), blk(2 * pairs),
        ],
        out_specs=pl.BlockSpec((None, S, LANES), lambda b, g: (b, 0, g)),
        out_shape=jax.ShapeDtypeStruct((B, S, A_HEADS * HEAD_DIM), BF16),
        scratch_shapes=[
            pltpu.VMEM((S, LANES), F32),
            pltpu.VMEM((MAX_WINDOW + S, LANES), F32),
            pltpu.VMEM((MAX_WINDOW + S, LANES), F32),
            pltpu.VMEM((_spread_row(S), LANES), F32),
            pltpu.VMEM((_spread_row(MAX_WINDOW + S), LANES), F32),
            pltpu.VMEM((_spread_row(MAX_WINDOW + S), LANES), F32),
            pltpu.VMEM((len(DILATED_CONFIGS) - 1, S, LANES), F32),
            pltpu.VMEM((len(DILATED_CONFIGS) - 1, S, LANES), F32),
            pltpu.VMEM((len(DILATED_CONFIGS) - 1, S, LANES), F32),
            pltpu.VMEM((3, _spread_row(S), LANES), F32),
            pltpu.VMEM((2 * len(DILATED_CONFIGS), 2 * BAND, 2 * BAND), F32),
        ],
        compiler_params=_params(("parallel", "parallel")),
        name="dilated_attn",
    )(slopes, proj3, proj3, proj3)


TQ = 512
KV_UNROLL = 2
DIFF_HEADS = 4


def _diff_kernel(slopes_ref, q_ref, k_ref, v_ref, lam_ref, sg_ref, o_ref,
                 m_ref, acc_ref, *, lam_init):
    hg = pl.program_id(1)
    qi = pl.program_id(2)
    lo = lax.broadcasted_iota(jnp.int32, (TQ, LANES), 1) < HEAD_DIM
    heads = range(DIFF_HEADS)
    cols = [slice(hh * LANES, (hh + 1) * LANES) for hh in heads]
    slopes = [slopes_ref[hg * DIFF_HEADS + hh] for hh in heads]
    q2 = []
    for hh in heads:
        q = q_ref[:, cols[hh]]
        zero = jnp.zeros_like(q)
        q2.append(jnp.concatenate([jnp.where(lo, q, zero), jnp.where(lo, zero, q)], axis=0))

    m_ref[...] = jnp.full(m_ref.shape, NEG_INF, F32)
    acc_ref[...] = jnp.zeros(acc_ref.shape, F32)

    def update(hh, key0, nk, row_blocks, masked):
        ks = pl.ds(pl.multiple_of(key0, nk), nk)
        k = k_ref[ks, cols[hh]]
        v1 = jnp.concatenate([v_ref[ks, cols[hh]], jnp.ones((nk, LANES), BF16)], axis=1)
        qs = jnp.concatenate([q2[hh][r0:r0 + rn] for r0, rn in row_blocks], axis=0)
        nrows = qs.shape[0]
        kpos = lax.broadcasted_iota(jnp.int32, (1, nk), 1).astype(F32) + key0.astype(F32)
        s = _dot_nt(qs, k) + slopes[hh] * kpos
        if masked:
            rn = row_blocks[0][1]
            rel = (lax.broadcasted_iota(jnp.int32, (nrows, nk), 0) % rn
                   - lax.broadcasted_iota(jnp.int32, (nrows, nk), 1))
            s = jnp.where(rel >= 0, s, NEG_INF)
        m_old = jnp.concatenate([m_ref[hh, r0:r0 + rn] for r0, rn in row_blocks], axis=0)
        a_old = jnp.concatenate([acc_ref[hh, r0:r0 + rn] for r0, rn in row_blocks], axis=0)
        m_new = jnp.maximum(m_old, jnp.max(s, axis=1, keepdims=True))
        alpha = jnp.exp2(m_old - m_new)
        e = jnp.exp2(s - jnp.concatenate([m_new] * (nk // LANES), axis=1))
        a_new = jnp.concatenate([alpha, alpha], axis=1) * a_old + _dot(e.astype(BF16), v1)
        at = 0
        for r0, rn in row_blocks:
            m_ref[hh, r0:r0 + rn] = m_new[at:at + rn]
            acc_ref[hh, r0:r0 + rn] = a_new[at:at + rn]
            at += rn

    half = TQ // 2

    def step(j, masked, hh):
        if not masked:
            update(hh, j * TQ, TQ, [(0, 2 * TQ)], False)
            return
        update(hh, j * TQ, half, [(0, TQ), (TQ, TQ)], True)
        update(hh, j * TQ + half, half, [(half, half), (TQ + half, half)], True)

    def body(jq, carry):
        for u in range(KV_UNROLL):
            for hh in heads:
                step(KV_UNROLL * jq + u, False, hh)
        return carry
    nq = qi // KV_UNROLL
    lax.fori_loop(0, nq, body, 0)

    rem = qi - nq * KV_UNROLL
    for r in range(KV_UNROLL):
        @pl.when(rem == r)
        def _(r=r):
            for u in range(r):
                for hh in heads:
                    step(qi - r + u, False, hh)
            for hh in heads:
                step(qi, True, hh)

    lq = lam_ref[...]
    lam = (jnp.exp(jnp.sum(lq[0:1] * lq[1:2], axis=1, keepdims=True))
           - jnp.exp(jnp.sum(lq[2:3] * lq[3:4], axis=1, keepdims=True)) + lam_init)
    for hh in heads:
        on = acc_ref[hh, :, 0:LANES] / acc_ref[hh, :, LANES:]
        o = on[:TQ] - lam * on[TQ:]
        o = _rms_rows(o, sg_ref[...]) * (1.0 - lam_init)
        o_ref[:, cols[hh]] = o.astype(o_ref.dtype)


def _diff_attention(proj3, slopes, lam_vecs, sub_gain, lam_init):
    B, S, _ = proj3.shape
    width = DIFF_HEADS * LANES
    groups = B_HEADS // DIFF_HEADS
    qoff = 3 * A_HEADS * HEAD_DIM // width
    koff = qoff + groups
    voff = koff + groups
    return pl.pallas_call(
        functools.partial(_diff_kernel, lam_init=lam_init),
        grid=(B, groups, S // TQ),
        in_specs=[
            pl.BlockSpec(memory_space=pltpu.SMEM),
            pl.BlockSpec((None, TQ, width), lambda b, h, i: (b, i, qoff + h)),
            pl.BlockSpec((None, S, width), lambda b, h, i: (b, 0, koff + h)),
            pl.BlockSpec((None, S, width), lambda b, h, i: (b, 0, voff + h)),
            pl.BlockSpec((4, HEAD_DIM), lambda b, h, i: (0, 0)),
            pl.BlockSpec((1, 2 * HEAD_DIM), lambda b, h, i: (0, 0)),
        ],
        out_specs=pl.BlockSpec((None, TQ, width), lambda b, h, i: (b, i, h)),
        out_shape=jax.ShapeDtypeStruct((B, S, B_HEADS * 2 * HEAD_DIM), BF16),
        scratch_shapes=[
            pltpu.VMEM((DIFF_HEADS, 2 * TQ, LANES), F32),
            pltpu.VMEM((DIFF_HEADS, 2 * TQ, 2 * LANES), F32),
        ],
        compiler_params=_params(("parallel", "parallel", "parallel")),
        name="diff_attn",
    )(slopes, proj3, proj3, proj3, lam_vecs, sub_gain)


FFN_CHUNK = 512


def _seg_pitch(seg):
    p = seg // SUBLANES + 1
    return SUBLANES * (p if p % 2 else p + 1)


def _to_segment_rows(src, stage, dst_ref):
    tm, C = src.shape
    seg = tm // SUBLANES
    pitch = _seg_pitch(seg)
    for n in range(C // LANES):
        cols = slice(n * LANES, (n + 1) * LANES)
        for s in range(SUBLANES):
            stage[n, s * pitch:s * pitch + seg, :] = src[s * seg:(s + 1) * seg, cols]
    for j in range(seg):
        dst_ref[j * SUBLANES:(j + 1) * SUBLANES, :] = jnp.concatenate(
            [stage[n, pl.ds(j, SUBLANES, stride=pitch), :] for n in range(C // LANES)], axis=1)


def _from_segment_rows(val, stage, dst_ref):
    tm, C = val.shape
    seg = tm // SUBLANES
    pitch = _seg_pitch(seg)
    for n in range(C // LANES):
        cols = slice(n * LANES, (n + 1) * LANES)
        for j in range(seg):
            stage[n, pl.ds(j, SUBLANES, stride=pitch), :] = val[j * SUBLANES:(j + 1) * SUBLANES, cols]
        for s in range(SUBLANES):
            dst_ref[s * seg:(s + 1) * seg, cols] = stage[n, s * pitch:s * pitch + seg, :]


def _segment_rows_ref(src, seg_in, stage, hseg):
    if seg_in:
        return src
    _to_segment_rows(src, stage, hseg)
    return hseg


def _store_segment_rows(val, seg_out, stage, o_ref):
    if seg_out:
        o_ref[...] = val
    else:
        _from_segment_rows(val, stage, o_ref)


def _segment_conv(u, carry_ref, cols, w_ref, b_ref, taps):
    tm = u.shape[0]
    first_sublane = lax.broadcasted_iota(jnp.int32, (SUBLANES, u.shape[1]), 0) == 0
    wrapped = []
    for i in range(1, taps):
        tail = u[tm - i * SUBLANES:tm - (i - 1) * SUBLANES]
        wrapped.append(jnp.where(first_sublane,
                                 pltpu.roll(carry_ref[i - 1, :, cols], 1, 0),
                                 pltpu.roll(tail, 1, 0)))
        carry_ref[i - 1, :, cols] = tail
    out = b_ref[:, cols] + w_ref[taps - 1:taps, cols] * u
    for k in range(1, taps):
        shifted = jnp.concatenate(wrapped[k - 1::-1] + [u[:tm - k * SUBLANES]], axis=0)
        out = out + w_ref[taps - 1 - k:taps - k, cols] * shifted
    return out


def _ffn_kernel(*refs, taps, with_attn, seg_in, seg_out):
    if with_attn:
        x_ref, a_ref, b_ref, wo_ref, *refs = refs
        na = a_ref.shape[1]
        h = x_ref[...] + _dot(a_ref[...], wo_ref[0:na, :]) + _dot(b_ref[...], wo_ref[na:, :])
    else:
        h, *refs = refs
    g_ref, wup_ref, cw_ref, cb_ref, wdn_ref, o_ref, stage, hseg, carry, act = refs
    F = wdn_ref.shape[0]

    @pl.when(pl.program_id(1) == 0)
    def _():
        carry[...] = jnp.zeros(carry.shape, F32)

    hs_ref = _segment_rows_ref(h, seg_in, stage, hseg)
    xn = _rms_rows(hs_ref[...], g_ref[...]).astype(BF16)
    for c in range(F // FFN_CHUNK):
        gc = slice(c * FFN_CHUNK, (c + 1) * FFN_CHUNK)
        vc = slice(F + c * FFN_CHUNK, F + (c + 1) * FFN_CHUNK)
        gg = _segment_conv(_dot(xn, wup_ref[:, gc]), carry, gc, cw_ref, cb_ref, taps)
        vv = _segment_conv(_dot(xn, wup_ref[:, vc]), carry, vc, cw_ref, cb_ref, taps)
        act[:, gc] = (_gelu(gg) * vv).astype(BF16)
    _store_segment_rows(hs_ref[...] + _dot(act[...], wdn_ref[...]), seg_out, stage, o_ref)


def _stage_shape(tm, C):
    return (C // LANES, SUBLANES * _seg_pitch(tm // SUBLANES), LANES)


def _resident(shape):
    return pl.BlockSpec(shape, lambda b, i: (0,) * len(shape), pipeline_mode=pl.Buffered(1))


SEQ_TILE = 512


def _conv_ffn(h3, g, wup, cw, cb, wdn, attn=None, seg_in=False, seg_out=False, tm=SEQ_TILE):
    assert not (seg_in and attn is not None)
    B, S, D = h3.shape
    F2 = wup.shape[1]
    taps = cw.shape[0]
    row_tile = lambda width: pl.BlockSpec((None, tm, width), lambda b, i: (b, i, 0))
    attn_args, attn_specs = (), []
    if attn is not None:
        oa, ob, wo = attn
        attn_args = (oa, ob, wo)
        attn_specs = [row_tile(oa.shape[2]), row_tile(ob.shape[2]), _resident(wo.shape)]
    return pl.pallas_call(
        functools.partial(_ffn_kernel, taps=taps, with_attn=attn is not None,
                          seg_in=seg_in, seg_out=seg_out),
        grid=(B, S // tm),
        in_specs=[
            row_tile(D),
            *attn_specs,
            _resident((1, D)),
            _resident(wup.shape),
            _resident(cw.shape),
            _resident((1, F2)),
            _resident(wdn.shape),
        ],
        out_specs=row_tile(D),
        out_shape=jax.ShapeDtypeStruct((B, S, D), F32),
        scratch_shapes=[
            pltpu.VMEM(_stage_shape(tm, D), F32),
            pltpu.VMEM((tm, D), F32),
            pltpu.VMEM((taps - 1, SUBLANES, F2), F32),
            pltpu.VMEM((tm, F2 // 2), BF16),
        ],
        compiler_params=_params(("arbitrary", "arbitrary")),
        name="conv_ffn",
    )(h3, *attn_args, g, wup, cw, cb, wdn)


def _rec_kernel(h_ref, g_ref, win_ref, cw_ref, cb_ref, wa_ref, ba_ref, wx_ref, bx_ref,
                ap_ref, wout_ref, o_ref, stage, hseg, carry, hstate, *, taps, seg_in, seg_out):
    tm, C = h_ref.shape
    seg = tm // SUBLANES
    bw = C // LRU_BLOCKS

    @pl.when(pl.program_id(1) == 0)
    def _():
        carry[...] = jnp.zeros(carry.shape, F32)
        hstate[...] = jnp.zeros(hstate.shape, F32)

    hs_ref = _segment_rows_ref(h_ref, seg_in, stage, hseg)
    xn = _rms_rows(hs_ref[...], g_ref[...]).astype(BF16)
    gate = _dot(xn, win_ref[:, 0:C])
    xr_raw = _dot(xn, win_ref[:, C:])

    ap = ap_ref[...]
    decay = -LRU_C * (jnp.maximum(-ap, 0.0) + jnp.log1p(jnp.exp(-jnp.abs(ap))))
    sublane = lax.broadcasted_iota(jnp.int32, (SUBLANES, bw), 0)
    blocks = []
    for n in range(LRU_BLOCKS):
        cols = slice(n * bw, (n + 1) * bw)
        xr = _segment_conv(xr_raw[:, cols], carry, cols, cw_ref, cb_ref, taps)
        xb = xr.astype(BF16)
        r = _sigmoid(_dot(xb, wa_ref[n]) + ba_ref[:, cols])
        i = _sigmoid(_dot(xb, wx_ref[n]) + bx_ref[:, cols])
        log_a = decay[:, cols] * r
        a = jnp.exp(log_a)
        u = jnp.sqrt(-jnp.tanh(log_a) * (1.0 + a * a)) * (i * xr)

        hl = jnp.zeros((SUBLANES, bw), F32)
        pp = jnp.ones((SUBLANES, bw), F32)
        hls, pps = [], []
        for j in range(seg):
            aj = a[j * SUBLANES:(j + 1) * SUBLANES]
            hl = aj * hl + u[j * SUBLANES:(j + 1) * SUBLANES]
            pp = aj * pp
            hls.append(hl)
            pps.append(pp)
        cin = hstate[0:1, cols]
        h_in = jnp.zeros((SUBLANES, bw), F32)
        for s in range(SUBLANES):
            h_in = jnp.where(sublane == s, cin, h_in)
            cin = hl[s:s + 1] + pp[s:s + 1] * cin
        hstate[:, cols] = jnp.broadcast_to(cin, (SUBLANES, bw))
        blocks.append(jnp.concatenate([hls[j] + pps[j] * h_in for j in range(seg)], axis=0))

    hs = jnp.concatenate(blocks, axis=1)
    y = (hs * _gelu(gate)).astype(BF16)
    _store_segment_rows(hs_ref[...] + _dot(y, wout_ref[...]), seg_out, stage, o_ref)


def _recurrent_block(h3, g, win, cw, cb, wa, ba, wx, bx, ap, wout,
                     seg_in=False, seg_out=False, tm=SEQ_TILE):
    B, S, D = h3.shape
    C = wout.shape[0]
    taps = cw.shape[0]
    unused = (SUBLANES, LANES)
    stage_shape = unused if (seg_in and seg_out) else _stage_shape(tm, D)
    hseg_shape = unused if seg_in else (tm, D)
    return pl.pallas_call(
        functools.partial(_rec_kernel, taps=taps, seg_in=seg_in, seg_out=seg_out),
        grid=(B, S // tm),
        in_specs=[
            pl.BlockSpec((None, tm, D), lambda b, i: (b, i, 0)),
            _resident((1, D)), _resident(win.shape), _resident(cw.shape), _resident((1, C)),
            _resident(wa.shape), _resident((1, C)), _resident(wx.shape), _resident((1, C)),
            _resident((1, C)), _resident(wout.shape),
        ],
        out_specs=pl.BlockSpec((None, tm, D), lambda b, i: (b, i, 0)),
        out_shape=jax.ShapeDtypeStruct((B, S, D), F32),
        scratch_shapes=[
            pltpu.VMEM(stage_shape, F32),
            pltpu.VMEM(hseg_shape, F32),
            pltpu.VMEM((taps - 1, SUBLANES, C), F32),
            pltpu.VMEM((SUBLANES, C), F32),
        ],
        compiler_params=_params(("arbitrary", "arbitrary")),
        name="recurrent_block",
    )(h3, g, win, cw, cb, wa, ba, wx, bx, ap, wout)


def _alibi_slopes(n):
    return jnp.exp2(-8.0 * jnp.arange(1, n + 1, dtype=F32) / n)


def _row(v):
    return v.reshape(1, -1).astype(F32)


def kernel(x, attn_norm, attn_w_in, attn_w_out, a_q_norm, a_k_norm, b_q_norm, b_k_norm, b_sub_norm,
           b_lam_q1, b_lam_k1, b_lam_q2, b_lam_k2, rec_norm, rec_w_in, rec_conv_w, rec_conv_b,
           rec_gate_a_w, rec_gate_a_b, rec_gate_x_w, rec_gate_x_b, rec_a_param, rec_w_out,
           ffn_norm, ffn_w_up, ffn_conv_w, ffn_conv_b, ffn_w_down):
    B, S, D = x.shape
    depth = ffn_norm.shape[0]
    slopes = _alibi_slopes(A_HEADS + B_HEADS)
    h = x
    h_is_seg = False
    for layer in range(depth):
        j = layer // 2
        attn = None
        if layer % 2 == 0:
            lam_init = 0.8 - 0.6 * math.exp(-0.3 * layer)
            scale = HEAD_DIM ** -0.5
            reps = SEC // HEAD_DIM
            ones = jnp.ones((SEC,), F32)
            head_gains = jnp.stack([
                jnp.tile(a_q_norm[j].astype(F32), reps) * (scale * LOG2E),
                jnp.tile(a_k_norm[j].astype(F32), reps), ones,
                jnp.tile(b_q_norm[j].astype(F32), reps) * (scale * LOG2E),
                jnp.tile(b_k_norm[j].astype(F32), reps), ones])
            proj = _attn_inproj(h.reshape(B * S, D), _row(attn_norm[j]),
                                attn_w_in[j].astype(BF16), head_gains)
            proj3 = proj.reshape(B, S, -1)
            oa = _dilated_attention(proj3, slopes[:A_HEADS] * LOG2E)
            lam_vecs = jnp.stack([b_lam_q1[j], b_lam_k1[j], b_lam_q2[j], b_lam_k2[j]]).astype(F32)
            ob = _diff_attention(proj3, slopes[A_HEADS:] * LOG2E, lam_vecs, _row(b_sub_norm[j]),
                                 lam_init)
            attn = (oa, ob, attn_w_out[j].astype(BF16))
        else:
            h = _recurrent_block(
                h, _row(rec_norm[j]), rec_w_in[j].astype(BF16), rec_conv_w[j].astype(F32),
                _row(rec_conv_b[j]), rec_gate_a_w[j].astype(BF16), _row(rec_gate_a_b[j]),
                rec_gate_x_w[j].astype(BF16), _row(rec_gate_x_b[j]), _row(rec_a_param[j]),
                rec_w_out[j].astype(BF16), seg_in=h_is_seg, seg_out=True)
            h_is_seg = True
        next_is_rec = layer + 1 < depth and (layer + 1) % 2 == 1
        h = _conv_ffn(h, _row(ffn_norm[layer]), ffn_w_up[layer].astype(BF16),
                      ffn_conv_w[layer].astype(F32), _row(ffn_conv_b[layer]),
                      ffn_w_down[layer].astype(BF16), attn=attn,
                      seg_in=h_is_seg, seg_out=next_is_rec)
        h_is_seg = next_is_rec
    return h
```
